```python
import math
import jax, jax.numpy as jnp
from jax import lax
import numpy as np

D_MODEL = 1024
BATCH = 32
SEQ = 2048
DEPTH = 4

D_RNN = 1536
RNN_BLOCKS = 12
RNN_BLOCK = D_RNN // RNN_BLOCKS
RG_C = 8.0
CONV_K = 4
DN_HEADS = 8
DN_DK = 128
DN_DV = 128
DN_QK = DN_HEADS * DN_DK
DN_V = DN_HEADS * DN_DV
DN_QKV = 2 * DN_QK + DN_V
CHUNK = 64
D_FF = ((8 * D_MODEL + 3 * 256 - 1) // (3 * 256)) * 256
EPS = 1e-6

N_BA = 4 * DN_HEADS
N_GATES = 2 * D_MODEL
SPLIT_IDX = [D_RNN, 2 * D_RNN, 2 * D_RNN + DN_QKV, 2 * D_RNN + DN_QKV + DN_V,
             2 * D_RNN + DN_QKV + DN_V + N_BA]
N_IN = 2 * D_RNN + DN_QKV + DN_V + N_BA + N_GATES

kernel_name = "hybrid_rglru_gdn_parallel_encoder"


def rmsnorm(x, g):
    xf = x.astype(jnp.float32)
    y = xf * lax.rsqrt(jnp.mean(xf * xf, axis=-1, keepdims=True) + EPS)
    return (y * g.astype(jnp.float32)).astype(x.dtype)


def dwconv_centred(x, w):
    C = x.shape[-1]
    pad_l = CONV_K // 2
    return lax.conv_general_dilated(
        x, w[:, None, :].astype(x.dtype), window_strides=(1,),
        padding=[(pad_l, CONV_K - 1 - pad_l)],
        dimension_numbers=("NWC", "WIO", "NWC"), feature_group_count=C)


def _lin_combine(e1, e2):
    a1, b1 = e1
    a2, b2 = e2
    return a1 * a2, a2 * b1 + b2


def rglru_branch(rx, ry, conv_w, conv_b, wa, ba, wx, bx, lam):
    Bn, S, _ = rx.shape
    u = (dwconv_centred(rx, conv_w) + conv_b.astype(rx.dtype)).astype(jnp.float32)
    ub = u.reshape(Bn, S, RNN_BLOCKS, RNN_BLOCK)
    r = jax.nn.sigmoid(jnp.einsum("bsnk,dnkj->dbsnj", ub, wa.astype(jnp.float32)).reshape(2, Bn, S, D_RNN)
                       + ba.astype(jnp.float32)[:, None, None, :])
    i = jax.nn.sigmoid(jnp.einsum("bsnk,dnkj->dbsnj", ub, wx.astype(jnp.float32)).reshape(2, Bn, S, D_RNN)
                       + bx.astype(jnp.float32)[:, None, None, :])
    log_a = -RG_C * jax.nn.softplus(-lam.astype(jnp.float32))[:, None, None, :] * r
    a = jnp.exp(log_a)
    b = jnp.sqrt(-jnp.expm1(2.0 * log_a)) * (i * u[None])
    h_fwd = lax.associative_scan(_lin_combine, (a[0], b[0]), axis=1)[1]
    h_bwd = lax.associative_scan(_lin_combine, (a[1], b[1]), axis=1, reverse=True)[1]
    y = (h_fwd + h_bwd) * jax.nn.gelu(ry.astype(jnp.float32))
    return y.astype(rx.dtype)


def chunk_gated_delta(q, k, v, g, beta):
    N, L, H, dk = q.shape
    dv = v.shape[-1]
    nc = L // CHUNK

    def to_chunks(t):
        return jnp.moveaxis(t.reshape(N, nc, CHUNK, H, *t.shape[3:]), 3, 1)

    q, k, v, g, beta = (to_chunks(t) for t in (q, k, v, g, beta))
    gc = jnp.cumsum(g, axis=-1)
    idx = jnp.arange(CHUNK)
    incl = idx[:, None] >= idx[None, :]
    strict = idx[:, None] > idx[None, :]
    diff = gc[..., :, None] - gc[..., None, :]
    decay = jnp.exp(jnp.where(incl, diff, -jnp.inf))
    kb = k * beta[..., None]
    A = jnp.where(strict, jnp.einsum("nhcid,nhcjd->nhcij", kb, k) * decay, 0.0)
    lhs = A + jnp.eye(CHUNK, dtype=A.dtype)
    u = lax.linalg.triangular_solve(lhs, v * beta[..., None], left_side=True, lower=True, unit_diagonal=True)
    w = lax.linalg.triangular_solve(lhs, kb * jnp.exp(gc)[..., None], left_side=True, lower=True, unit_diagonal=True)
    att = jnp.einsum("nhcid,nhcjd->nhcij", q, k) * decay
    qg = q * jnp.exp(gc)[..., None]
    g_last = gc[..., -1]
    kd = k * jnp.exp(g_last[..., None] - gc)[..., None]

    def step(S, inp):
        qg_c, kd_c, u_c, w_c, att_c, gl_c = inp
        v_new = u_c - jnp.einsum("nhck,nhkv->nhcv", w_c, S)
        o = jnp.einsum("nhck,nhkv->nhcv", qg_c, S) + jnp.einsum("nhij,nhjv->nhiv", att_c, v_new)
        S = S * jnp.exp(gl_c)[..., None, None] + jnp.einsum("nhck,nhcv->nhkv", kd_c, v_new)
        return S, o

    xs = tuple(jnp.moveaxis(t, 2, 0) for t in (qg, kd, u, w, att, g_last))
    S0 = jnp.zeros((N, H, dk, dv), jnp.float32)
    _, o = lax.scan(step, S0, xs)
    return jnp.transpose(o, (1, 0, 3, 2, 4)).reshape(N, L, H, dv)


def deltanet_branch(qkv, z, ba, conv_w, a_log, dt_bias, norm_g):
    Bn, S, _ = qkv.shape
    c = jax.nn.silu(dwconv_centred(qkv, conv_w)).astype(jnp.float32)
    q, k, v = jnp.split(c, [DN_QK, 2 * DN_QK], axis=-1)
    q = q.reshape(Bn, S, DN_HEADS, DN_DK)
    k = k.reshape(Bn, S, DN_HEADS, DN_DK)
    v = v.reshape(Bn, S, DN_HEADS, DN_DV)
    q = q * lax.rsqrt(jnp.sum(q * q, -1, keepdims=True) + EPS) * (DN_DK ** -0.5)
    k = k * lax.rsqrt(jnp.sum(k * k, -1, keepdims=True) + EPS)
    bal = ba.astype(jnp.float32).reshape(Bn, S, 2, 2, DN_HEADS)
    beta = jax.nn.sigmoid(bal[:, :, 0])
    g = -jnp.exp(a_log.astype(jnp.float32)) * jax.nn.softplus(bal[:, :, 1] + dt_bias.astype(jnp.float32))
    def both(t_f, t_b):
        return jnp.concatenate([t_f, jnp.flip(t_b, 1)], axis=0)
    o = chunk_gated_delta(both(q, q), both(k, k), both(v, v),
                          both(g[:, :, 0], g[:, :, 1]), both(beta[:, :, 0], beta[:, :, 1]))
    o = o[:Bn] + jnp.flip(o[Bn:], 1)
    o = o * lax.rsqrt(jnp.mean(o * o, -1, keepdims=True) + EPS) * norm_g.astype(jnp.float32)
    o = o * jax.nn.silu(z.astype(jnp.float32).reshape(Bn, S, DN_HEADS, DN_DV))
    return o.reshape(Bn, S, DN_V).astype(qkv.dtype)


def setup_inputs(seed: int = 0) -> dict:
    key = jax.random.key(seed)
    ks = jax.random.split(key, 24)
    f32 = jnp.float32
    nrm = lambda k, shape, s: jax.random.normal(k, shape, f32) * s
    lam_u = jax.random.uniform(ks[9], (DEPTH, 2, D_RNN), f32, 0.9, 0.999) ** (1.0 / RG_C)
    dt = jnp.exp(jax.random.uniform(ks[14], (DEPTH, 2, DN_HEADS), f32, math.log(1e-3), math.log(1e-1)))
    return {
        "x": jax.random.normal(ks[0], (BATCH, SEQ, D_MODEL), f32),
        "mix_norm": 1.0 + nrm(ks[1], (DEPTH, D_MODEL), 0.02),
        "w_in": nrm(ks[2], (DEPTH, D_MODEL, N_IN), D_MODEL ** -0.5),
        "rg_conv_w": nrm(ks[3], (DEPTH, CONV_K, D_RNN), CONV_K ** -0.5),
        "rg_conv_b": nrm(ks[4], (DEPTH, D_RNN), 0.01),
        "rg_wa": nrm(ks[5], (DEPTH, 2, RNN_BLOCKS, RNN_BLOCK, RNN_BLOCK), RNN_BLOCK ** -0.5),
        "rg_ba": nrm(ks[6], (DEPTH, 2, D_RNN), 0.01),
        "rg_wx": nrm(ks[7], (DEPTH, 2, RNN_BLOCKS, RNN_BLOCK, RNN_BLOCK), RNN_BLOCK ** -0.5),
        "rg_bx": nrm(ks[8], (DEPTH, 2, D_RNN), 0.01),
        "rg_lambda": jnp.log(lam_u) - jnp.log1p(-lam_u),
        "w_rnn_proj": nrm(ks[10], (DEPTH, D_RNN, D_MODEL), D_RNN ** -0.5),
        "dn_conv_w": nrm(ks[11], (DEPTH, CONV_K, DN_QKV), CONV_K ** -0.5),
        "dn_a_log": jnp.log(jax.random.uniform(ks[12], (DEPTH, 2, DN_HEADS), f32, 1.0, 16.0)),
        "dn_dt_bias": jnp.log(jnp.expm1(dt)),
        "dn_norm": 1.0 + nrm(ks[13], (DEPTH, DN_DV), 0.02),
        "w_dn_proj": nrm(ks[15], (DEPTH, DN_V, D_MODEL), DN_V ** -0.5),
        "w_out": nrm(ks[16], (DEPTH, D_MODEL, D_MODEL), D_MODEL ** -0.5),
        "ffn_norm": 1.0 + nrm(ks[17], (DEPTH, D_MODEL), 0.02),
        "w_gate_up": nrm(ks[18], (DEPTH, D_MODEL, 2 * D_FF), D_MODEL ** -0.5),
        "w_down": nrm(ks[19], (DEPTH, D_FF, D_MODEL), D_FF ** -0.5),
        "final_norm": 1.0 + nrm(ks[20], (D_MODEL,), 0.02),
    }


def reference(x, mix_norm, w_in, rg_conv_w, rg_conv_b, rg_wa, rg_ba, rg_wx, rg_bx, rg_lambda,
              w_rnn_proj, dn_conv_w, dn_a_log, dn_dt_bias, dn_norm, w_dn_proj, w_out,
              ffn_norm, w_gate_up, w_down, final_norm):
    for l in range(DEPTH):
        h = rmsnorm(x, mix_norm[l])
        p = h @ w_in[l]
        rx, ry, qkv, z, ba, gates = jnp.split(p, SPLIT_IDX, axis=-1)
        y_rnn = rglru_branch(rx, ry, rg_conv_w[l], rg_conv_b[l], rg_wa[l], rg_ba[l],
                             rg_wx[l], rg_bx[l], rg_lambda[l]) @ w_rnn_proj[l]
        y_dn = deltanet_branch(qkv, z, ba, dn_conv_w[l], dn_a_log[l], dn_dt_bias[l],
                               dn_norm[l]) @ w_dn_proj[l]
        g_rnn, g_dn = jnp.split(jax.nn.sigmoid(gates), 2, axis=-1)
        x = x + (g_rnn * y_rnn + g_dn * y_dn) @ w_out[l]
        h = rmsnorm(x, ffn_norm[l])
        gu = h @ w_gate_up[l]
        gt, up = jnp.split(gu, 2, axis=-1)
        x = x + (jax.nn.silu(gt) * up) @ w_down[l]
    return rmsnorm(x, final_norm)
```

```python
import functools

import jax
import jax.numpy as jnp
from jax import lax
from jax.experimental import pallas as pl
from jax.experimental.pallas import tpu as pltpu

F32 = jnp.float32
BF16 = jnp.bfloat16

D_MODEL = 1024
D_RNN = 1536
RNN_BLOCK = 128
RNN_BLOCKS = D_RNN // RNN_BLOCK
RG_C = 8.0
CONV_K = 4
DN_HEADS = 8
DN_DK = 128
DN_DV = 128
DN_QK = DN_HEADS * DN_DK
DN_V = DN_HEADS * DN_DV
DN_QKV = 2 * DN_QK + DN_V
CHUNK = 64
D_FF = 2816
EPS = 1e-6
N_BA = 4 * DN_HEADS

LANES = 128
SUBLANES = 8

COL_RX = 0
COL_RY = D_RNN
COL_QKV = 2 * D_RNN
COL_Z = COL_QKV + DN_QKV
COL_GATES = COL_Z + DN_V
COL_BA = COL_GATES + 2 * D_MODEL
N_P = COL_BA + LANES

VMEM_LIMIT = 56 * 1024 * 1024


def _sigmoid(x):
    return 1.0 / (1.0 + jnp.exp(-x))


def _softplus(x):
    return jnp.maximum(x, 0.0) + jnp.log(1.0 + jnp.exp(-jnp.abs(x)))


def _dot(a, b):
    return jnp.dot(a, b, preferred_element_type=F32)


def _dot_nt(a, b):
    return lax.dot_general(a, b, (((1,), (1,)), ((), ())), preferred_element_type=F32)


def _dot_tn(a, b):
    return lax.dot_general(a, b, (((0,), (0,)), ((), ())), preferred_element_type=F32)


def _lane_sum(x, ones_bf16):
    hi = x.astype(BF16)
    lo = (x - hi.astype(F32)).astype(BF16)
    return _dot(hi, ones_bf16) + _dot(lo, ones_bf16)


def _shift_rows(x, rows, k):
    n = x.shape[0]
    rolled = pltpu.roll(x, k % n, axis=0)
    if k > 0:
        return jnp.where(rows >= k, rolled, 0.0)
    return jnp.where(rows < n + k, rolled, 0.0)


def _dwconv(x, rows, cw):
    return (cw[0:1] * _shift_rows(x, rows, 2) + cw[1:2] * _shift_rows(x, rows, 1)
            + cw[2:3] * x + cw[3:4] * _shift_rows(x, rows, -1))


def _in_proj_kernel(x_ref, g_ref, w_ref, o_ref, *, col_chunk):
    x = x_ref[...]
    ms = jnp.mean(x * x, axis=-1, keepdims=True)
    h = (x * lax.rsqrt(ms + EPS) * g_ref[...]).astype(BF16)
    n = o_ref.shape[1]
    for c0 in range(0, n, col_chunk):
        cn = min(col_chunk, n - c0)
        o_ref[:, c0:c0 + cn] = _dot(h, w_ref[:, c0:c0 + cn]).astype(o_ref.dtype)


def _in_proj(x2d, gain, w, *, tm):
    t, d = x2d.shape
    n = w.shape[1]
    return pl.pallas_call(
        functools.partial(_in_proj_kernel, col_chunk=1024),
        grid=(t // tm,),
        in_specs=[
            pl.BlockSpec((tm, d), lambda i: (i, 0)),
            pl.BlockSpec((1, d), lambda i: (0, 0)),
            pl.BlockSpec((d, n), lambda i: (0, 0)),
        ],
        out_specs=pl.BlockSpec((tm, n), lambda i: (i, 0)),
        out_shape=jax.ShapeDtypeStruct((t, n), BF16),
        compiler_params=pltpu.CompilerParams(
            dimension_semantics=("arbitrary",), vmem_limit_bytes=VMEM_LIMIT),
        name="in_proj",
    )(x2d, gain, w)


GP_BETA, GP_GC, GP_EGC, GP_BEGC, GP_EGR, GP_EGT = range(6)
GP_GROUP = 2 * DN_HEADS


def _gate_prep_kernel(ba_ref, alog_ref, dtb_ref, gp_ref, gt_ref):
    s = ba_ref.shape[1]
    x = ba_ref[0].astype(F32)
    lane = lax.broadcasted_iota(jnp.int32, (s, LANES), 1)
    row = lax.broadcasted_iota(jnp.int32, (s, LANES), 0)
    ic = row & (CHUNK - 1)
    beta = _sigmoid(x)
    xs = pltpu.roll(x, LANES - GP_GROUP, axis=1)
    g = -jnp.exp(alog_ref[...]) * _softplus(xs + dtb_ref[...])
    fsum = g
    rsum = g
    sh = 1
    while sh < CHUNK:
        fsum = fsum + jnp.where(ic >= sh, pltpu.roll(fsum, sh, axis=0), 0.0)
        rsum = rsum + jnp.where(ic < CHUNK - sh, pltpu.roll(rsum, s - sh, axis=0), 0.0)
        sh *= 2
    is_fwd = (lane & (GP_GROUP - 1)) < DN_HEADS
    gc = jnp.where(is_fwd, fsum, rsum)
    gr = jnp.where(is_fwd, rsum, fsum) - g
    gtot = fsum + rsum - g
    egc = jnp.exp(gc)
    parts = (beta, gc, egc, beta * egc, jnp.exp(gr), jnp.exp(gtot))
    out = parts[-1]
    out = pltpu.roll(out, (len(parts) - 1) * GP_GROUP, axis=1)
    for qi in range(len(parts) - 2, -1, -1):
        placed = parts[qi] if qi == 0 else pltpu.roll(parts[qi], qi * GP_GROUP, axis=1)
        out = jnp.where(lane < (qi + 1) * GP_GROUP, placed, out)
    gp_ref[0] = out
    out_t = out.T
    gt_ref[0] = out_t[GP_GC * GP_GROUP:(GP_GC + 1) * GP_GROUP, :]


def _gate_prep(p3, alog_row, dtb_row):
    b, s, _ = p3.shape
    return pl.pallas_call(
        _gate_prep_kernel,
        grid=(b,),
        in_specs=[
            pl.BlockSpec((1, s, LANES), lambda i: (i, 0, COL_BA // LANES)),
            pl.BlockSpec((1, LANES), lambda i: (0, 0)),
            pl.BlockSpec((1, LANES), lambda i: (0, 0)),
        ],
        out_specs=[
            pl.BlockSpec((1, s, LANES), lambda i: (i, 0, 0)),
            pl.BlockSpec((1, GP_GROUP, s), lambda i: (i, 0, 0)),
        ],
        out_shape=[
            jax.ShapeDtypeStruct((b, s, LANES), F32),
            jax.ShapeDtypeStruct((b, GP_GROUP, s), F32),
        ],
        compiler_params=pltpu.CompilerParams(
            dimension_semantics=("arbitrary",), vmem_limit_bytes=VMEM_LIMIT),
        name="gate_prep",
    )(p3, alog_row, dtb_row)


RG_GROUP = SUBLANES
RG_SLAB_PAD = 8
RG_TB = 32


def _rglru_kernel(rx_ref, ry_ref, cw_ref, cb_ref, w_ref, bias_ref, lam_ref, o_ref, u_s, h_s):
    s = rx_ref.shape[1]
    pitch = s + RG_SLAB_PAD
    nb = s // RG_TB
    rows = lax.broadcasted_iota(jnp.int32, (s, LANES), 0)
    cw = cw_ref[...]
    cb = cb_ref[...]

    def stage(b, carry):
        x = rx_ref[b].astype(F32)
        u_s[pl.ds(pl.multiple_of(b * pitch, SUBLANES), s), :] = _dwconv(x, rows, cw) + cb
        return carry

    lax.fori_loop(0, RG_GROUP, stage, 0)

    decay_scale = -RG_C * _softplus(-lam_ref[...])
    bias = bias_ref[...]

    def gather(ref, t0):
        return [ref[pl.ds(t0 + j, RG_GROUP, stride=pitch), :] for j in range(RG_TB)]

    def scan_block(t0, h, d, accumulate):
        u_rows = gather(u_s, t0)
        u = jnp.concatenate(u_rows, axis=0)
        gts = _dot(u.astype(BF16), w_ref[:, d * 2 * LANES:(d + 1) * 2 * LANES])
        gts = gts + bias[:, d * 2 * LANES:(d + 1) * 2 * LANES]
        r = _sigmoid(gts[:, :LANES])
        i = _sigmoid(gts[:, LANES:])
        log_a = decay_scale[d:d + 1] * r
        a = jnp.exp(log_a)
        bb = jnp.sqrt(1.0 - a * a) * (i * u)
        order = range(RG_TB) if d == 0 else range(RG_TB - 1, -1, -1)
        for j in order:
            sl = slice(j * RG_GROUP, (j + 1) * RG_GROUP)
            h = a[sl] * h + bb[sl]
            dst = pl.ds(t0 + j, RG_GROUP, stride=pitch)
            if accumulate:
                h_s[dst, :] = h_s[dst, :] + h
            else:
                h_s[dst, :] = h
        return h

    def make_body(accumulate):
        def body(i, carry):
            hf, hb = carry
            hf = scan_block(i * RG_TB, hf, 0, accumulate)
            hb = scan_block((nb - 1 - i) * RG_TB, hb, 1, accumulate)
            return hf, hb
        return body

    zero = jnp.zeros((RG_GROUP, LANES), F32)
    carry = lax.fori_loop(0, nb // 2, make_body(False), (zero, zero))
    lax.fori_loop(nb // 2, nb, make_body(True), carry)

    def finish(b, carry):
        hsum = h_s[pl.ds(pl.multiple_of(b * pitch, SUBLANES), s), :]
        y = hsum * jax.nn.gelu(ry_ref[b].astype(F32))
        o_ref[b] = y.astype(o_ref.dtype)
        return carry

    lax.fori_loop(0, RG_GROUP, finish, 0)


def _rglru(p3, cw, cb, w, bias, lam):
    b, s, _ = p3.shape
    pitch = s + RG_SLAB_PAD
    blk = (RG_GROUP, s, LANES)
    return pl.pallas_call(
        _rglru_kernel,
        grid=(b // RG_GROUP, RNN_BLOCKS),
        in_specs=[
            pl.BlockSpec(blk, lambda g, n: (g, 0, COL_RX // LANES + n)),
            pl.BlockSpec(blk, lambda g, n: (g, 0, COL_RY // LANES + n)),
            pl.BlockSpec((CONV_K, LANES), lambda g, n: (0, n)),
            pl.BlockSpec((1, LANES), lambda g, n: (0, n)),
            pl.BlockSpec((None, RNN_BLOCK, 4 * RNN_BLOCK), lambda g, n: (n, 0, 0)),
            pl.BlockSpec((None, 1, 4 * RNN_BLOCK), lambda g, n: (n, 0, 0)),
            pl.BlockSpec((2, LANES), lambda g, n: (0, n)),
        ],
        out_specs=pl.BlockSpec(blk, lambda g, n: (g, 0, n)),
        out_shape=jax.ShapeDtypeStruct((b, s, D_RNN), BF16),
        scratch_shapes=[
            pltpu.VMEM((RG_GROUP * pitch, LANES), F32),
            pltpu.VMEM((RG_GROUP * pitch, LANES), F32),
        ],
        compiler_params=pltpu.CompilerParams(
            dimension_semantics=("arbitrary", "arbitrary"), vmem_limit_bytes=VMEM_LIMIT),
        name="rglru",
    )(p3, p3, cw, cb, w, bias, lam)


def _deltanet_kernel(q_ref, k_ref, v_ref, z_ref, gp_ref, gt_ref, cwq_ref, cwk_ref, cwv_ref, ng_ref,
                     o_ref, qn_s, kn_s, kb_s, vb_s, kbe_s, qg_s, kd_s, gc_s, egt_s, w_s, u_s, att_s, o_s):
    s = q_ref.shape[1]
    nc = s // CHUNK
    head = pl.program_id(1)
    rows = lax.broadcasted_iota(jnp.int32, (s, LANES), 0)
    ones = jnp.ones((LANES, LANES), BF16)

    def conv_silu(ref, cw_ref):
        y = _dwconv(ref[0].astype(F32), rows, cw_ref[...])
        return y * _sigmoid(y)

    q = conv_silu(q_ref, cwq_ref)
    k = conv_silu(k_ref, cwk_ref)
    v = conv_silu(v_ref, cwv_ref)
    qn = q * lax.rsqrt(_lane_sum(q * q, ones) + EPS) * (DN_DK ** -0.5)
    kn = k * lax.rsqrt(_lane_sum(k * k, ones) + EPS)
    qn_s[...] = qn.astype(BF16)
    kn_s[...] = kn.astype(BF16)

    gp = pltpu.roll(gp_ref[0], (LANES - head) % LANES, axis=1)

    def col(quantity, d):
        c = quantity * GP_GROUP + d * DN_HEADS
        return jnp.broadcast_to(gp[:, c:c + 1], (s, LANES))

    for d in range(2):
        beta = col(GP_BETA, d)
        kb_s[d] = (kn * beta).astype(BF16)
        vb_s[d] = (v * beta).astype(BF16)
        kbe_s[d] = (kn * col(GP_BEGC, d)).astype(BF16)
        qg_s[d] = (qn * col(GP_EGC, d)).astype(BF16)
        kd_s[d] = (kn * col(GP_EGR, d)).astype(BF16)
        gc_s[d] = col(GP_GC, d)
        egt_s[d] = col(GP_EGT, d)

    ii = lax.broadcasted_iota(jnp.int32, (CHUNK, CHUNK), 0)
    jj = lax.broadcasted_iota(jnp.int32, (CHUNK, CHUNK), 1)
    eye = (ii == jj).astype(F32)

    def prep(c, carry):
        r0 = pl.multiple_of(c * CHUNK, CHUNK)
        knc = kn_s[pl.ds(r0, CHUNK), :]
        qnc = qn_s[pl.ds(r0, CHUNK), :]
        for d in range(2):
            incl = (ii >= jj) if d == 0 else (ii <= jj)
            strict = (ii > jj) if d == 0 else (ii < jj)
            gcol = gc_s[d, pl.ds(r0, CHUNK), 0:CHUNK]
            grow = gt_ref[0, pl.ds(d * DN_HEADS + head, 1), pl.ds(c, 1), :]
            decay = jnp.where(incl, jnp.exp(gcol - grow[0]), 0.0)
            kbc = kb_s[d, pl.ds(r0, CHUNK), :]
            a_mat = jnp.where(strict, _dot_nt(kbc, knc) * decay, 0.0)
            att = _dot_nt(qnc, knc) * decay
            att_s[d, pl.ds(r0, CHUNK), :] = att.astype(BF16)
            xp = -a_mat
            tinv = eye + xp
            for _ in range(5):
                xpb = xp.astype(BF16)
                xp = _dot(xpb, xpb)
                tinv = tinv + _dot(tinv.astype(BF16), xp.astype(BF16))
            rhs = jnp.concatenate([vb_s[d, pl.ds(r0, CHUNK), :], kbe_s[d, pl.ds(r0, CHUNK), :]], axis=1)
            uw = _dot(tinv.astype(BF16), rhs)
            u_s[d, pl.ds(r0, CHUNK), :] = uw[:, :DN_DV]
            w_s[d, pl.ds(r0, CHUNK), :] = uw[:, DN_DV:].astype(BF16)
        return carry

    lax.fori_loop(0, nc, prep, 0)

    def chain(i, states):
        new_states = []
        for d in range(2):
            c = i if d == 0 else nc - 1 - i
            r0 = pl.multiple_of(c * CHUNK, CHUNK)
            st = states[d]
            wq = jnp.concatenate([w_s[d, pl.ds(r0, CHUNK), :], qg_s[d, pl.ds(r0, CHUNK), :]], axis=0)
            res = _dot(wq, st.astype(BF16))
            v_new = (u_s[d, pl.ds(r0, CHUNK), :] - res[:CHUNK]).astype(BF16)
            o = res[CHUNK:] + _dot(att_s[d, pl.ds(r0, CHUNK), :], v_new)
            o_s[d, pl.ds(r0, CHUNK), :] = o
            st = st * egt_s[d, pl.ds(r0, 1), :] + _dot_tn(kd_s[d, pl.ds(r0, CHUNK), :], v_new)
            new_states.append(st)
        return tuple(new_states)

    zero = jnp.zeros((DN_DK, DN_DV), F32)
    lax.fori_loop(0, nc, chain, (zero, zero))

    o = o_s[0] + o_s[1]
    var = _lane_sum(o * o, ones) * (1.0 / DN_DV)
    z = z_ref[0].astype(F32)
    y = o * lax.rsqrt(var + EPS) * ng_ref[...] * (z * _sigmoid(z))
    o_ref[0] = y.astype(o_ref.dtype)


def _deltanet(p3, gp, gt4, cw, ng):
    b, s, _ = p3.shape
    blk = (1, s, LANES)
    qkv0 = COL_QKV // LANES
    seq_bf16 = lambda: pltpu.VMEM((2, s, LANES), BF16)
    seq_f32 = lambda: pltpu.VMEM((2, s, LANES), F32)
    return pl.pallas_call(
        _deltanet_kernel,
        grid=(b, DN_HEADS),
        in_specs=[
            pl.BlockSpec(blk, lambda i, h: (i, 0, qkv0 + h)),
            pl.BlockSpec(blk, lambda i, h: (i, 0, qkv0 + DN_HEADS + h)),
            pl.BlockSpec(blk, lambda i, h: (i, 0, qkv0 + 2 * DN_HEADS + h)),
            pl.BlockSpec(blk, lambda i, h: (i, 0, COL_Z // LANES + h)),
            pl.BlockSpec((1, s, LANES), lambda i, h: (i, 0, 0)),
            pl.BlockSpec((1, GP_GROUP, s // CHUNK, CHUNK), lambda i, h: (i, 0, 0, 0)),
            pl.BlockSpec((CONV_K, LANES), lambda i, h: (0, h)),
            pl.BlockSpec((CONV_K, LANES), lambda i, h: (0, DN_HEADS + h)),
            pl.BlockSpec((CONV_K, LANES), lambda i, h: (0, 2 * DN_HEADS + h)),
            pl.BlockSpec((1, LANES), lambda i, h: (0, 0)),
        ],
        out_specs=pl.BlockSpec(blk, lambda i, h: (i, 0, h)),
        out_shape=jax.ShapeDtypeStruct((b, s, DN_V), BF16),
        scratch_shapes=[
            pltpu.VMEM((s, LANES), BF16),
            pltpu.VMEM((s, LANES), BF16),
            seq_bf16(),
            seq_bf16(),
            seq_bf16(),
            seq_bf16(),
            seq_bf16(),
            seq_f32(),
            seq_f32(),
            seq_bf16(),
            seq_f32(),
            pltpu.VMEM((2, s, CHUNK), BF16),
            seq_f32(),
        ],
        compiler_params=pltpu.CompilerParams(
            dimension_semantics=("arbitrary", "arbitrary"), vmem_limit_bytes=VMEM_LIMIT),
        name="deltanet",
    )(p3, p3, p3, p3, gp, gt4, cw, cw, cw, ng)


FF_CHUNK = 256


def _merge_ffn_kernel(x_ref, yr_ref, yd_ref, gr_ref, gd_ref, wr_ref, wd_ref, wo_ref, fn_ref, wgu_ref,
                      wdn_ref, o_ref):
    y_rnn = _dot(yr_ref[...], wr_ref[...])
    y_dn = _dot(yd_ref[...], wd_ref[...])
    merged = (_sigmoid(gr_ref[...].astype(F32)) * y_rnn + _sigmoid(gd_ref[...].astype(F32)) * y_dn)
    x1 = x_ref[...] + _dot(merged.astype(BF16), wo_ref[...])
    ms = jnp.mean(x1 * x1, axis=-1, keepdims=True)
    h = (x1 * lax.rsqrt(ms + EPS) * fn_ref[...]).astype(BF16)
    acc = x1
    for c0 in range(0, D_FF, FF_CHUNK):
        gt = _dot(h, wgu_ref[:, c0:c0 + FF_CHUNK])
        up = _dot(h, wgu_ref[:, D_FF + c0:D_FF + c0 + FF_CHUNK])
        act = (gt * _sigmoid(gt) * up).astype(BF16)
        acc = acc + _dot(act, wdn_ref[c0:c0 + FF_CHUNK, :])
    o_ref[...] = acc


def _merge_ffn(x2d, y_rnn, y_dn, p2d, w_rnn, w_dn, w_out, ffn_gain, w_gu, w_down, *, tm):
    t, d = x2d.shape
    full = lambda a: pl.BlockSpec(a.shape, lambda i: (0,) * a.ndim)
    return pl.pallas_call(
        _merge_ffn_kernel,
        grid=(t // tm,),
        in_specs=[
            pl.BlockSpec((tm, d), lambda i: (i, 0)),
            pl.BlockSpec((tm, D_RNN), lambda i: (i, 0)),
            pl.BlockSpec((tm, DN_V), lambda i: (i, 0)),
            pl.BlockSpec((tm, d), lambda i: (i, COL_GATES // D_MODEL)),
            pl.BlockSpec((tm, d), lambda i: (i, COL_GATES // D_MODEL + 1)),
            full(w_rnn), full(w_dn), full(w_out), full(ffn_gain), full(w_gu), full(w_down),
        ],
        out_specs=pl.BlockSpec((tm, d), lambda i: (i, 0)),
        out_shape=jax.ShapeDtypeStruct((t, d), F32),
        compiler_params=pltpu.CompilerParams(
            dimension_semantics=("arbitrary",), vmem_limit_bytes=VMEM_LIMIT),
        name="merge_ffn",
    )(x2d, y_rnn, y_dn, p2d, p2d, w_rnn, w_dn, w_out, ffn_gain, w_gu, w_down)


def _final_norm_kernel(x_ref, g_ref, o_ref):
    x = x_ref[...]
    ms = jnp.mean(x * x, axis=-1, keepdims=True)
    o_ref[...] = x * lax.rsqrt(ms + EPS) * g_ref[...]


def _final_norm(x2d, gain, *, tm):
    t, d = x2d.shape
    return pl.pallas_call(
        _final_norm_kernel,
        grid=(t // tm,),
        in_specs=[pl.BlockSpec((tm, d), lambda i: (i, 0)), pl.BlockSpec((1, d), lambda i: (0, 0))],
        out_specs=pl.BlockSpec((tm, d), lambda i: (i, 0)),
        out_shape=jax.ShapeDtypeStruct((t, d), F32),
        compiler_params=pltpu.CompilerParams(dimension_semantics=("arbitrary",)),
        name="final_norm",
    )(x2d, gain)


def _pack_w_in(w_in_l):
    split_ba = COL_Z + DN_V
    pad = jnp.zeros((D_MODEL, LANES - N_BA), w_in_l.dtype)
    return jnp.concatenate(
        [w_in_l[:, :split_ba], w_in_l[:, split_ba + N_BA:], w_in_l[:, split_ba:split_ba + N_BA], pad],
        axis=1).astype(BF16)


def _pack_rg_gates(wa, wx, ba, bx):
    w = jnp.concatenate([wa[0], wx[0], wa[1], wx[1]], axis=-1).astype(BF16)
    blocks = lambda v: v.reshape(RNN_BLOCKS, 1, RNN_BLOCK)
    bias = jnp.concatenate([blocks(ba[0]), blocks(bx[0]), blocks(ba[1]), blocks(bx[1])], axis=-1)
    return w, bias.astype(F32)


def _gate_rows(a_log, dt_bias):
    pad = jnp.zeros((LANES - GP_GROUP,), F32)
    row = lambda v: jnp.concatenate([v.reshape(-1).astype(F32), pad]).reshape(1, LANES)
    return row(a_log), row(dt_bias)


def kernel(x, mix_norm, w_in, rg_conv_w, rg_conv_b, rg_wa, rg_ba, rg_wx, rg_bx, rg_lambda, w_rnn_proj,
           dn_conv_w, dn_a_log, dn_dt_bias, dn_norm, w_dn_proj, w_out, ffn_norm, w_gate_up, w_down,
           final_norm):
    b, s, d = x.shape
    depth = w_in.shape[0]
    t = b * s
    assert d == D_MODEL and b % RG_GROUP == 0 and s % (2 * RG_TB) == 0 and s % CHUNK == 0
    tm = 512 if t % 512 == 0 else 256
    x2d = x.reshape(t, d)
    for l in range(depth):
        p2d = _in_proj(x2d, mix_norm[l].reshape(1, d), _pack_w_in(w_in[l]), tm=tm)
        p3 = p2d.reshape(b, s, N_P)
        alog_row, dtb_row = _gate_rows(dn_a_log[l], dn_dt_bias[l])
        gp, gt = _gate_prep(p3, alog_row, dtb_row)
        gt4 = gt.reshape(b, GP_GROUP, s // CHUNK, CHUNK)
        rg_w, rg_bias = _pack_rg_gates(rg_wa[l], rg_wx[l], rg_ba[l], rg_bx[l])
        y_rnn = _rglru(p3, rg_conv_w[l], rg_conv_b[l].reshape(1, D_RNN), rg_w, rg_bias, rg_lambda[l])
        y_dn = _deltanet(p3, gp, gt4, dn_conv_w[l], dn_norm[l].reshape(1, DN_DV))
        x2d = _merge_ffn(
            x2d, y_rnn.reshape(t, D_RNN), y_dn.reshape(t, DN_V), p2d,
            w_rnn_proj[l].astype(BF16), w_dn_proj[l].astype(BF16), w_out[l].astype(BF16),
            ffn_norm[l].reshape(1, d), w_gate_up[l].astype(BF16), w_down[l].astype(BF16), tm=tm)
    out = _final_norm(x2d, final_norm.reshape(1, d), tm=tm)
    return out.reshape(b, s, d)
```

```python
import functools

import jax
import jax.numpy as jnp
from jax import lax
from jax.experimental import pallas as pl
from jax.experimental.pallas import tpu as pltpu

F32 = jnp.float32
BF16 = jnp.bfloat16

D_MODEL = 1024
D_RNN = 1536
RNN_BLOCK = 128
RNN_BLOCKS = D_RNN // RNN_BLOCK
RG_C = 8.0
CONV_K = 4
DN_HEADS = 8
DN_DK = 128
DN_DV = 128
DN_QK = DN_HEADS * DN_DK
DN_V = DN_HEADS * DN_DV
DN_QKV = 2 * DN_QK + DN_V
CHUNK = 64
D_FF = 2816
EPS = 1e-6
N_BA = 4 * DN_HEADS

LANES = 128
SUBLANES = 8

COL_RX = 0
COL_RY = D_RNN
COL_QKV = 2 * D_RNN
COL_Z = COL_QKV + DN_QKV
COL_GATES = COL_Z + DN_V
COL_BA = COL_GATES + 2 * D_MODEL
N_P = COL_BA + LANES

VMEM_LIMIT = 56 * 1024 * 1024


def _sigmoid(x):
    return 1.0 / (1.0 + jnp.exp(-x))


def _softplus(x):
    return jnp.maximum(x, 0.0) + jnp.log(1.0 + jnp.exp(-jnp.abs(x)))


def _dot(a, b):
    return jnp.dot(a, b, preferred_element_type=F32)


def _dot_nt(a, b):
    return lax.dot_general(a, b, (((1,), (1,)), ((), ())), preferred_element_type=F32)


def _dot_tn(a, b):
    return lax.dot_general(a, b, (((0,), (0,)), ((), ())), preferred_element_type=F32)


def _lane_sum(x, ones_bf16):
    hi = x.astype(BF16)
    lo = (x - hi.astype(F32)).astype(BF16)
    return _dot(hi, ones_bf16) + _dot(lo, ones_bf16)


def _shift_rows(x, rows, k):
    n = x.shape[0]
    rolled = pltpu.roll(x, k % n, axis=0)
    if k > 0:
        return jnp.where(rows >= k, rolled, 0.0)
    return jnp.where(rows < n + k, rolled, 0.0)


def _dwconv(x, rows, cw):
    return (cw[0:1] * _shift_rows(x, rows, 2) + cw[1:2] * _shift_rows(x, rows, 1)
            + cw[2:3] * x + cw[3:4] * _shift_rows(x, rows, -1))


def _in_proj_kernel(x_ref, g_ref, w_ref, o_ref, *, col_chunk):
    x = x_ref[...]
    ms = jnp.mean(x * x, axis=-1, keepdims=True)
    h = (x * lax.rsqrt(ms + EPS) * g_ref[...]).astype(BF16)
    n = o_ref.shape[1]
    for c0 in range(0, n, col_chunk):
        cn = min(col_chunk, n - c0)
        o_ref[:, c0:c0 + cn] = _dot(h, w_ref[:, c0:c0 + cn]).astype(o_ref.dtype)


def _in_proj(x2d, gain, w, *, tm):
    t, d = x2d.shape
    n = w.shape[1]
    return pl.pallas_call(
        functools.partial(_in_proj_kernel, col_chunk=1024),
        grid=(t // tm,),
        in_specs=[
            pl.BlockSpec((tm, d), lambda i: (i, 0)),
            pl.BlockSpec((1, d), lambda i: (0, 0)),
            pl.BlockSpec((d, n), lambda i: (0, 0)),
        ],
        out_specs=pl.BlockSpec((tm, n), lambda i: (i, 0)),
        out_shape=jax.ShapeDtypeStruct((t, n), BF16),
        compiler_params=pltpu.CompilerParams(
            dimension_semantics=("arbitrary",), vmem_limit_bytes=VMEM_LIMIT),
        name="in_proj",
    )(x2d, gain, w)


GP_BETA, GP_GC, GP_EGC, GP_BEGC, GP_EGR, GP_EGT = range(6)
GP_GROUP = 2 * DN_HEADS


def _gate_prep_kernel(ba_ref, alog_ref, dtb_ref, gp_ref, gt_ref):
    s = ba_ref.shape[1]
    x = ba_ref[0].astype(F32)
    lane = lax.broadcasted_iota(jnp.int32, (s, LANES), 1)
    row = lax.broadcasted_iota(jnp.int32, (s, LANES), 0)
    ic = row & (CHUNK - 1)
    beta = _sigmoid(x)
    xs = pltpu.roll(x, LANES - GP_GROUP, axis=1)
    g = -jnp.exp(alog_ref[...]) * _softplus(xs + dtb_ref[...])
    fsum = g
    rsum = g
    sh = 1
    while sh < CHUNK:
        fsum = fsum + jnp.where(ic >= sh, pltpu.roll(fsum, sh, axis=0), 0.0)
        rsum = rsum + jnp.where(ic < CHUNK - sh, pltpu.roll(rsum, s - sh, axis=0), 0.0)
        sh *= 2
    is_fwd = (lane & (GP_GROUP - 1)) < DN_HEADS
    gc = jnp.where(is_fwd, fsum, rsum)
    gr = jnp.where(is_fwd, rsum, fsum) - g
    gtot = fsum + rsum - g
    egc = jnp.exp(gc)
    parts = (beta, gc, egc, beta * egc, jnp.exp(gr), jnp.exp(gtot))
    out = parts[-1]
    out = pltpu.roll(out, (len(parts) - 1) * GP_GROUP, axis=1)
    for qi in range(len(parts) - 2, -1, -1):
        placed = parts[qi] if qi == 0 else pltpu.roll(parts[qi], qi * GP_GROUP, axis=1)
        out = jnp.where(lane < (qi + 1) * GP_GROUP, placed, out)
    gp_ref[0] = out
    out_t = out.T
    gt_ref[0] = out_t[GP_GC * GP_GROUP:(GP_GC + 1) * GP_GROUP, :]


def _gate_prep(p3, alog_row, dtb_row):
    b, s, _ = p3.shape
    return pl.pallas_call(
        _gate_prep_kernel,
        grid=(b,),
        in_specs=[
            pl.BlockSpec((1, s, LANES), lambda i: (i, 0, COL_BA // LANES)),
            pl.BlockSpec((1, LANES), lambda i: (0, 0)),
            pl.BlockSpec((1, LANES), lambda i: (0, 0)),
        ],
        out_specs=[
            pl.BlockSpec((1, s, LANES), lambda i: (i, 0, 0)),
            pl.BlockSpec((1, GP_GROUP, s), lambda i: (i, 0, 0)),
        ],
        out_shape=[
            jax.ShapeDtypeStruct((b, s, LANES), F32),
            jax.ShapeDtypeStruct((b, GP_GROUP, s), F32),
        ],
        compiler_params=pltpu.CompilerParams(
            dimension_semantics=("arbitrary",), vmem_limit_bytes=VMEM_LIMIT),
        name="gate_prep",
    )(p3, alog_row, dtb_row)


RG_GROUP = SUBLANES
RG_SLAB_PAD = 8
RG_TB = 32


def _rglru_kernel(rx_ref, ry_ref, cw_ref, cb_ref, w_ref, bias_ref, lam_ref, o_ref, u_s, h_s):
    s = rx_ref.shape[1]
    pitch = s + RG_SLAB_PAD
    nb = s // RG_TB
    rows = lax.broadcasted_iota(jnp.int32, (s, LANES), 0)
    cw = cw_ref[...]
    cb = cb_ref[...]

    def stage(b, carry):
        x = rx_ref[b].astype(F32)
        u_s[pl.ds(pl.multiple_of(b * pitch, SUBLANES), s), :] = _dwconv(x, rows, cw) + cb
        return carry

    lax.fori_loop(0, RG_GROUP, stage, 0)

    decay_scale = -RG_C * _softplus(-lam_ref[...])
    bias = bias_ref[...]

    def gather(ref, t0):
        return [ref[pl.ds(t0 + j, RG_GROUP, stride=pitch), :] for j in range(RG_TB)]

    def scan_block(t0, h, d, accumulate):
        u_rows = gather(u_s, t0)
        u = jnp.concatenate(u_rows, axis=0)
        gts = _dot(u.astype(BF16), w_ref[:, d * 2 * LANES:(d + 1) * 2 * LANES])
        gts = gts + bias[:, d * 2 * LANES:(d + 1) * 2 * LANES]
        r = _sigmoid(gts[:, :LANES])
        i = _sigmoid(gts[:, LANES:])
        log_a = decay_scale[d:d + 1] * r
        a = jnp.exp(log_a)
        bb = jnp.sqrt(1.0 - a * a) * (i * u)
        order = range(RG_TB) if d == 0 else range(RG_TB - 1, -1, -1)
        for j in order:
            sl = slice(j * RG_GROUP, (j + 1) * RG_GROUP)
            h = a[sl] * h + bb[sl]
            dst = pl.ds(t0 + j, RG_GROUP, stride=pitch)
            if accumulate:
                h_s[dst, :] = h_s[dst, :] + h
            else:
                h_s[dst, :] = h
        return h

    def make_body(accumulate):
        def body(i, carry):
            hf, hb = carry
            hf = scan_block(i * RG_TB, hf, 0, accumulate)
            hb = scan_block((nb - 1 - i) * RG_TB, hb, 1, accumulate)
            return hf, hb
        return body

    zero = jnp.zeros((RG_GROUP, LANES), F32)
    carry = lax.fori_loop(0, nb // 2, make_body(False), (zero, zero))
    lax.fori_loop(nb // 2, nb, make_body(True), carry)

    def finish(b, carry):
        hsum = h_s[pl.ds(pl.multiple_of(b * pitch, SUBLANES), s), :]
        y = hsum * jax.nn.gelu(ry_ref[b].astype(F32))
        o_ref[b] = y.astype(o_ref.dtype)
        return carry

    lax.fori_loop(0, RG_GROUP, finish, 0)


def _rglru(p3, cw, cb, w, bias, lam):
    b, s, _ = p3.shape
    pitch = s + RG_SLAB_PAD
    blk = (RG_GROUP, s, LANES)
    return pl.pallas_call(
        _rglru_kernel,
        grid=(b // RG_GROUP, RNN_BLOCKS),
        in_specs=[
            pl.BlockSpec(blk, lambda g, n: (g, 0, COL_RX // LANES + n)),
            pl.BlockSpec(blk, lambda g, n: (g, 0, COL_RY // LANES + n)),
            pl.BlockSpec((CONV_K, LANES), lambda g, n: (0, n)),
            pl.BlockSpec((1, LANES), lambda g, n: (0, n)),
            pl.BlockSpec((None, RNN_BLOCK, 4 * RNN_BLOCK), lambda g, n: (n, 0, 0)),
            pl.BlockSpec((None, 1, 4 * RNN_BLOCK), lambda g, n: (n, 0, 0)),
            pl.BlockSpec((2, LANES), lambda g, n: (0, n)),
        ],
        out_specs=pl.BlockSpec(blk, lambda g, n: (g, 0, n)),
        out_shape=jax.ShapeDtypeStruct((b, s, D_RNN), BF16),
        scratch_shapes=[
            pltpu.VMEM((RG_GROUP * pitch, LANES), F32),
            pltpu.VMEM((RG_GROUP * pitch, LANES), F32),
        ],
        compiler_params=pltpu.CompilerParams(
            dimension_semantics=("arbitrary", "arbitrary"), vmem_limit_bytes=VMEM_LIMIT),
        name="rglru",
    )(p3, p3, cw, cb, w, bias, lam)


DN_HB = 2
DN_G = 4


def _deltanet_kernel(q_ref, k_ref, v_ref, z_ref, gp_ref, gt_ref, cwq_ref, cwk_ref, cwv_ref, ng_ref,
                     o_ref, qn_s, kn_s, kb_s, vb_s, kbe_s, qg_s, kd_s, gpr_s, w_s, u_s, att_s, o_s):
    s = q_ref.shape[1]
    nc = s // CHUNK
    head0 = pl.program_id(1) * DN_HB
    rows = lax.broadcasted_iota(jnp.int32, (s, LANES), 0)
    ones = jnp.ones((LANES, LANES), BF16)

    for hh in range(DN_HB):
        lanes = slice(hh * LANES, (hh + 1) * LANES)

        def conv_silu(ref, cw_ref):
            y = _dwconv(ref[0, :, lanes].astype(F32), rows, cw_ref[:, lanes])
            return y * _sigmoid(y)

        q = conv_silu(q_ref, cwq_ref)
        k = conv_silu(k_ref, cwk_ref)
        v = conv_silu(v_ref, cwv_ref)
        qn = q * lax.rsqrt(_lane_sum(q * q, ones) + EPS) * (DN_DK ** -0.5)
        kn = k * lax.rsqrt(_lane_sum(k * k, ones) + EPS)
        qn_s[hh] = qn.astype(BF16)
        kn_s[hh] = kn.astype(BF16)
        gp = pltpu.roll(gp_ref[0], (LANES - (head0 + hh)) % LANES, axis=1)
        gpr_s[hh] = gp

        def col(quantity, d):
            c = quantity * GP_GROUP + d * DN_HEADS
            return jnp.broadcast_to(gp[:, c:c + 1], (s, LANES))

        for d in range(2):
            sid = hh * 2 + d
            beta = col(GP_BETA, d)
            kb_s[sid] = (kn * beta).astype(BF16)
            vb_s[sid] = (v * beta).astype(BF16)
            kbe_s[sid] = (kn * col(GP_BEGC, d)).astype(BF16)
            qg_s[sid] = (qn * col(GP_EGC, d)).astype(BF16)
            kd_s[sid] = (kn * col(GP_EGR, d)).astype(BF16)

    ii = lax.broadcasted_iota(jnp.int32, (CHUNK, CHUNK), 0)
    jj = lax.broadcasted_iota(jnp.int32, (CHUNK, CHUNK), 1)
    eye = (ii == jj).astype(F32)
    incl = ((ii >= jj), (ii <= jj))
    strict = ((ii > jj), (ii < jj))

    def prep(it, carry):
        chains = []
        for hh in range(DN_HB):
            for g in range(DN_G):
                c = it * DN_G + g
                r0 = pl.multiple_of(c * CHUNK, CHUNK)
                for d in range(2):
                    chains.append((hh, d, c, r0))
        kk, att, decay = [], [], []
        for hh, d, c, r0 in chains:
            sid = hh * 2 + d
            knc = kn_s[hh, pl.ds(r0, CHUNK), :]
            gc_lane = GP_GC * GP_GROUP + d * DN_HEADS
            gcol = jnp.broadcast_to(gpr_s[hh, pl.ds(r0, CHUNK), gc_lane:gc_lane + 1], (CHUNK, CHUNK))
            grow = gt_ref[0, pl.ds(d * DN_HEADS + head0 + hh, 1), pl.ds(c, 1), :]
            decay.append(jnp.where(incl[d], jnp.exp(gcol - grow[0]), 0.0))
            kk.append(_dot_nt(kb_s[sid, pl.ds(r0, CHUNK), :], knc))
            att.append(_dot_nt(qn_s[hh, pl.ds(r0, CHUNK), :], knc))
        xp, tinv = [], []
        for n, (hh, d, c, r0) in enumerate(chains):
            sid = hh * 2 + d
            att_s[sid, pl.ds(r0, CHUNK), :] = (att[n] * decay[n]).astype(BF16)
            x = jnp.where(strict[d], -(kk[n] * decay[n]), 0.0)
            xp.append(x)
            tinv.append(eye + x)
        for _ in range(5):
            xpb = [x.astype(BF16) for x in xp]
            xp = [_dot(x, x) for x in xpb]
            tinv = [t + _dot(t.astype(BF16), x.astype(BF16)) for t, x in zip(tinv, xp)]
        for n, (hh, d, c, r0) in enumerate(chains):
            sid = hh * 2 + d
            rhs = jnp.concatenate([vb_s[sid, pl.ds(r0, CHUNK), :], kbe_s[sid, pl.ds(r0, CHUNK), :]], axis=1)
            uw = _dot(tinv[n].astype(BF16), rhs)
            u_s[sid, pl.ds(r0, CHUNK), :] = uw[:, :DN_DV]
            w_s[sid, pl.ds(r0, CHUNK), :] = uw[:, DN_DV:].astype(BF16)
        return carry

    lax.fori_loop(0, nc // DN_G, prep, 0)

    streams = [(hh, d) for hh in range(DN_HB) for d in range(2)]

    def chain(i, states):
        r0s = [pl.multiple_of((i if d == 0 else nc - 1 - i) * CHUNK, CHUNK) for _, d in streams]
        res = []
        for (hh, d), r0, st in zip(streams, r0s, states):
            sid = hh * 2 + d
            wq = jnp.concatenate([w_s[sid, pl.ds(r0, CHUNK), :], qg_s[sid, pl.ds(r0, CHUNK), :]], axis=0)
            res.append(_dot(wq, st.astype(BF16)))
        new_states = []
        for (hh, d), r0, st, rs in zip(streams, r0s, states, res):
            sid = hh * 2 + d
            v_new = (u_s[sid, pl.ds(r0, CHUNK), :] - rs[:CHUNK]).astype(BF16)
            o_s[sid, pl.ds(r0, CHUNK), :] = rs[CHUNK:] + _dot(att_s[sid, pl.ds(r0, CHUNK), :], v_new)
            egt_lane = GP_EGT * GP_GROUP + d * DN_HEADS
            egt = gpr_s[hh, pl.ds(r0, 1), egt_lane:egt_lane + 1]
            new_states.append(st * egt + _dot_tn(kd_s[sid, pl.ds(r0, CHUNK), :], v_new))
        return tuple(new_states)

    zero = jnp.zeros((DN_DK, DN_DV), F32)
    lax.fori_loop(0, nc, chain, tuple(zero for _ in streams))

    for hh in range(DN_HB):
        lanes = slice(hh * LANES, (hh + 1) * LANES)
        o = o_s[hh * 2] + o_s[hh * 2 + 1]
        var = _lane_sum(o * o, ones) * (1.0 / DN_DV)
        z = z_ref[0, :, lanes].astype(F32)
        y = o * lax.rsqrt(var + EPS) * ng_ref[...] * (z * _sigmoid(z))
        o_ref[0, :, lanes] = y.astype(o_ref.dtype)


def _deltanet(p3, gp, gt4, cw, ng):
    b, s, _ = p3.shape
    width = DN_HB * LANES
    blk = (1, s, width)
    hblocks = DN_HEADS // DN_HB
    qkv0 = COL_QKV // width
    ns = 2 * DN_HB
    seq_bf16 = lambda: pltpu.VMEM((ns, s, LANES), BF16)
    seq_f32 = lambda: pltpu.VMEM((ns, s, LANES), F32)
    return pl.pallas_call(
        _deltanet_kernel,
        grid=(b, hblocks),
        in_specs=[
            pl.BlockSpec(blk, lambda i, h: (i, 0, qkv0 + h)),
            pl.BlockSpec(blk, lambda i, h: (i, 0, qkv0 + hblocks + h)),
            pl.BlockSpec(blk, lambda i, h: (i, 0, qkv0 + 2 * hblocks + h)),
            pl.BlockSpec(blk, lambda i, h: (i, 0, COL_Z // width + h)),
            pl.BlockSpec((1, s, LANES), lambda i, h: (i, 0, 0)),
            pl.BlockSpec((1, GP_GROUP, s // CHUNK, CHUNK), lambda i, h: (i, 0, 0, 0)),
            pl.BlockSpec((CONV_K, width), lambda i, h: (0, h)),
            pl.BlockSpec((CONV_K, width), lambda i, h: (0, hblocks + h)),
            pl.BlockSpec((CONV_K, width), lambda i, h: (0, 2 * hblocks + h)),
            pl.BlockSpec((1, LANES), lambda i, h: (0, 0)),
        ],
        out_specs=pl.BlockSpec(blk, lambda i, h: (i, 0, h)),
        out_shape=jax.ShapeDtypeStruct((b, s, DN_V), BF16),
        scratch_shapes=[
            pltpu.VMEM((DN_HB, s, LANES), BF16),
            pltpu.VMEM((DN_HB, s, LANES), BF16),
            seq_bf16(),
            seq_bf16(),
            seq_bf16(),
            seq_bf16(),
            seq_bf16(),
            pltpu.VMEM((DN_HB, s, LANES), F32),
            seq_bf16(),
            seq_f32(),
            pltpu.VMEM((ns, s, CHUNK), BF16),
            seq_f32(),
        ],
        compiler_params=pltpu.CompilerParams(
            dimension_semantics=("arbitrary", "arbitrary"), vmem_limit_bytes=VMEM_LIMIT),
        name="deltanet",
    )(p3, p3, p3, p3, gp, gt4, cw, cw, cw, ng)


FF_CHUNK = 256


def _merge_ffn_kernel(x_ref, yr_ref, yd_ref, gr_ref, gd_ref, wr_ref, wd_ref, wo_ref, fn_ref, wgu_ref,
                      wdn_ref, o_ref):
    y_rnn = _dot(yr_ref[...], wr_ref[...])
    y_dn = _dot(yd_ref[...], wd_ref[...])
    merged = (_sigmoid(gr_ref[...].astype(F32)) * y_rnn + _sigmoid(gd_ref[...].astype(F32)) * y_dn)
    x1 = x_ref[...] + _dot(merged.astype(BF16), wo_ref[...])
    ms = jnp.mean(x1 * x1, axis=-1, keepdims=True)
    h = (x1 * lax.rsqrt(ms + EPS) * fn_ref[...]).astype(BF16)
    acc = x1
    for c0 in range(0, D_FF, FF_CHUNK):
        gt = _dot(h, wgu_ref[:, c0:c0 + FF_CHUNK])
        up = _dot(h, wgu_ref[:, D_FF + c0:D_FF + c0 + FF_CHUNK])
        act = (gt * _sigmoid(gt) * up).astype(BF16)
        acc = acc + _dot(act, wdn_ref[c0:c0 + FF_CHUNK, :])
    o_ref[...] = acc


def _merge_ffn(x2d, y_rnn, y_dn, p2d, w_rnn, w_dn, w_out, ffn_gain, w_gu, w_down, *, tm):
    t, d = x2d.shape
    full = lambda a: pl.BlockSpec(a.shape, lambda i: (0,) * a.ndim)
    return pl.pallas_call(
        _merge_ffn_kernel,
        grid=(t // tm,),
        in_specs=[
            pl.BlockSpec((tm, d), lambda i: (i, 0)),
            pl.BlockSpec((tm, D_RNN), lambda i: (i, 0)),
            pl.BlockSpec((tm, DN_V), lambda i: (i, 0)),
            pl.BlockSpec((tm, d), lambda i: (i, COL_GATES // D_MODEL)),
            pl.BlockSpec((tm, d), lambda i: (i, COL_GATES // D_MODEL + 1)),
            full(w_rnn), full(w_dn), full(w_out), full(ffn_gain), full(w_gu), full(w_down),
        ],
        out_specs=pl.BlockSpec((tm, d), lambda i: (i, 0)),
        out_shape=jax.ShapeDtypeStruct((t, d), F32),
        compiler_params=pltpu.CompilerParams(
            dimension_semantics=("arbitrary",), vmem_limit_bytes=VMEM_LIMIT),
        name="merge_ffn",
    )(x2d, y_rnn, y_dn, p2d, p2d, w_rnn, w_dn, w_out, ffn_gain, w_gu, w_down)


def _final_norm_kernel(x_ref, g_ref, o_ref):
    x = x_ref[...]
    ms = jnp.mean(x * x, axis=-1, keepdims=True)
    o_ref[...] = x * lax.rsqrt(ms + EPS) * g_ref[...]


def _final_norm(x2d, gain, *, tm):
    t, d = x2d.shape
    return pl.pallas_call(
        _final_norm_kernel,
        grid=(t // tm,),
        in_specs=[pl.BlockSpec((tm, d), lambda i: (i, 0)), pl.BlockSpec((1, d), lambda i: (0, 0))],
        out_specs=pl.BlockSpec((tm, d), lambda i: (i, 0)),
        out_shape=jax.ShapeDtypeStruct((t, d), F32),
        compiler_params=pltpu.CompilerParams(dimension_semantics=("arbitrary",)),
        name="final_norm",
    )(x2d, gain)


def _pack_w_in(w_in_l):
    split_ba = COL_Z + DN_V
    pad = jnp.zeros((D_MODEL, LANES - N_BA), w_in_l.dtype)
    return jnp.concatenate(
        [w_in_l[:, :split_ba], w_in_l[:, split_ba + N_BA:], w_in_l[:, split_ba:split_ba + N_BA], pad],
        axis=1).astype(BF16)


def _pack_rg_gates(wa, wx, ba, bx):
    w = jnp.concatenate([wa[0], wx[0], wa[1], wx[1]], axis=-1).astype(BF16)
    blocks = lambda v: v.reshape(RNN_BLOCKS, 1, RNN_BLOCK)
    bias = jnp.concatenate([blocks(ba[0]), blocks(bx[0]), blocks(ba[1]), blocks(bx[1])], axis=-1)
    return w, bias.astype(F32)


def _gate_rows(a_log, dt_bias):
    pad = jnp.zeros((LANES - GP_GROUP,), F32)
    row = lambda v: jnp.concatenate([v.reshape(-1).astype(F32), pad]).reshape(1, LANES)
    return row(a_log), row(dt_bias)


def kernel(x, mix_norm, w_in, rg_conv_w, rg_conv_b, rg_wa, rg_ba, rg_wx, rg_bx, rg_lambda, w_rnn_proj,
           dn_conv_w, dn_a_log, dn_dt_bias, dn_norm, w_dn_proj, w_out, ffn_norm, w_gate_up, w_down,
           final_norm):
    b, s, d = x.shape
    depth = w_in.shape[0]
    t = b * s
    assert d == D_MODEL and b % RG_GROUP == 0 and s % (2 * RG_TB) == 0 and s % (DN_G * CHUNK) == 0
    tm = 512 if t % 512 == 0 else 256
    x2d = x.reshape(t, d)
    for l in range(depth):
        p2d = _in_proj(x2d, mix_norm[l].reshape(1, d), _pack_w_in(w_in[l]), tm=tm)
        p3 = p2d.reshape(b, s, N_P)
        alog_row, dtb_row = _gate_rows(dn_a_log[l], dn_dt_bias[l])
        gp, gt = _gate_prep(p3, alog_row, dtb_row)
        gt4 = gt.reshape(b, GP_GROUP, s // CHUNK, CHUNK)
        rg_w, rg_bias = _pack_rg_gates(rg_wa[l], rg_wx[l], rg_ba[l], rg_bx[l])
        y_rnn = _rglru(p3, rg_conv_w[l], rg_conv_b[l].reshape(1, D_RNN), rg_w, rg_bias, rg_lambda[l])
        y_dn = _deltanet(p3, gp, gt4, dn_conv_w[l], dn_norm[l].reshape(1, DN_DV))
        x2d = _merge_ffn(
            x2d, y_rnn.reshape(t, D_RNN), y_dn.reshape(t, DN_V), p2d,
            w_rnn_proj[l].astype(BF16), w_dn_proj[l].astype(BF16), w_out[l].astype(BF16),
            ffn_norm[l].reshape(1, d), w_gate_up[l].astype(BF16), w_down[l].astype(BF16), tm=tm)
    out = _final_norm(x2d, final_norm.reshape(1, d), tm=tm)
    return out.reshape(b, s, d)
```

```python
import functools

import jax
import jax.numpy as jnp
from jax import lax
from jax.experimental import pallas as pl
from jax.experimental.pallas import tpu as pltpu

F32 = jnp.float32
BF16 = jnp.bfloat16

D_MODEL = 1024
D_RNN = 1536
RNN_BLOCK = 128
RNN_BLOCKS = D_RNN // RNN_BLOCK
RG_C = 8.0
CONV_K = 4
DN_HEADS = 8
DN_DK = 128
DN_DV = 128
DN_QK = DN_HEADS * DN_DK
DN_V = DN_HEADS * DN_DV
DN_QKV = 2 * DN_QK + DN_V
CHUNK = 64
D_FF = 2816
EPS = 1e-6
N_BA = 4 * DN_HEADS
LOG2_E = 1.4426950408889634
TINY = 1e-30

LANES = 128
SUBLANES = 8

COL_RX = 0
COL_RY = D_RNN
COL_QKV = 2 * D_RNN
COL_Z = COL_QKV + DN_QKV
COL_GATES = COL_Z + DN_V
COL_BA = COL_GATES + 2 * D_MODEL
N_P = COL_BA + LANES

VMEM_LIMIT = 56 * 1024 * 1024


def _sigmoid(x):
    return 1.0 / (1.0 + jnp.exp(-x))


def _softplus(x):
    return jnp.maximum(x, 0.0) + jnp.log(1.0 + jnp.exp(-jnp.abs(x)))


def _dot(a, b):
    return jnp.dot(a, b, preferred_element_type=F32)


def _dot_nt(a, b):
    return lax.dot_general(a, b, (((1,), (1,)), ((), ())), preferred_element_type=F32)


def _dot_tn(a, b):
    return lax.dot_general(a, b, (((0,), (0,)), ((), ())), preferred_element_type=F32)


def _lane_sum(x, ones_bf16):
    return _dot(x.astype(BF16), ones_bf16)


CONV_HALO = SUBLANES


def _zero_halo(pad_ref, s):
    zeros = jnp.zeros((CONV_HALO, pad_ref.shape[1]), F32)
    pad_ref[0:CONV_HALO, :] = zeros
    pad_ref[CONV_HALO + s:2 * CONV_HALO + s, :] = zeros


def _dwconv(pad_ref, x, cw):
    s = x.shape[0]
    pad_ref[CONV_HALO:CONV_HALO + s, :] = x
    y = cw[0:1] * pad_ref[CONV_HALO - 2:CONV_HALO - 2 + s, :]
    for j in range(1, CONV_K):
        y = y + cw[j:j + 1] * pad_ref[CONV_HALO - 2 + j:CONV_HALO - 2 + j + s, :]
    return y


def _in_proj_kernel(x_ref, g_ref, w_ref, o_ref, *, col_chunk):
    x = x_ref[...]
    ms = jnp.mean(x * x, axis=-1, keepdims=True)
    h = (x * lax.rsqrt(ms + EPS) * g_ref[...]).astype(BF16)
    n = o_ref.shape[1]
    for c0 in range(0, n, col_chunk):
        cn = min(col_chunk, n - c0)
        o_ref[:, c0:c0 + cn] = _dot(h, w_ref[:, c0:c0 + cn]).astype(o_ref.dtype)


def _in_proj(x2d, gain, w, *, tm):
    t, d = x2d.shape
    n = w.shape[1]
    return pl.pallas_call(
        functools.partial(_in_proj_kernel, col_chunk=1024),
        grid=(t // tm,),
        in_specs=[
            pl.BlockSpec((tm, d), lambda i: (i, 0)),
            pl.BlockSpec((1, d), lambda i: (0, 0)),
            pl.BlockSpec((d, n), lambda i: (0, 0)),
        ],
        out_specs=pl.BlockSpec((tm, n), lambda i: (i, 0)),
        out_shape=jax.ShapeDtypeStruct((t, n), BF16),
        compiler_params=pltpu.CompilerParams(
            dimension_semantics=("arbitrary",), vmem_limit_bytes=VMEM_LIMIT),
        name="in_proj",
    )(x2d, gain, w)


GP_BETA, GP_GC, GP_EGC, GP_BEGC, GP_EGR, GP_EGT = range(6)
GP_GROUP = 2 * DN_HEADS


def _gate_prep_kernel(ba_ref, alog_ref, dtb_ref, gp_ref, gt_ref):
    s = ba_ref.shape[1]
    x = ba_ref[0].astype(F32)
    lane = lax.broadcasted_iota(jnp.int32, (s, LANES), 1)
    row = lax.broadcasted_iota(jnp.int32, (s, LANES), 0)
    ic = row & (CHUNK - 1)
    beta = _sigmoid(x)
    xs = pltpu.roll(x, LANES - GP_GROUP, axis=1)
    g = -jnp.exp(alog_ref[...]) * _softplus(xs + dtb_ref[...])
    fsum = g
    rsum = g
    sh = 1
    while sh < CHUNK:
        fsum = fsum + jnp.where(ic >= sh, pltpu.roll(fsum, sh, axis=0), 0.0)
        rsum = rsum + jnp.where(ic < CHUNK - sh, pltpu.roll(rsum, s - sh, axis=0), 0.0)
        sh *= 2
    is_fwd = (lane & (GP_GROUP - 1)) < DN_HEADS
    gc = jnp.where(is_fwd, fsum, rsum)
    gr = jnp.where(is_fwd, rsum, fsum) - g
    gtot = fsum + rsum - g
    egc = jnp.exp(gc)
    parts = (beta, gc, egc, beta * egc, jnp.exp(gr), jnp.exp(gtot))
    out = parts[-1]
    out = pltpu.roll(out, (len(parts) - 1) * GP_GROUP, axis=1)
    for qi in range(len(parts) - 2, -1, -1):
        placed = parts[qi] if qi == 0 else pltpu.roll(parts[qi], qi * GP_GROUP, axis=1)
        out = jnp.where(lane < (qi + 1) * GP_GROUP, placed, out)
    gp_ref[0] = out
    out_t = out.T
    gt_ref[0] = out_t[GP_GC * GP_GROUP:(GP_GC + 1) * GP_GROUP, :]


def _gate_prep(p3, alog_row, dtb_row):
    b, s, _ = p3.shape
    return pl.pallas_call(
        _gate_prep_kernel,
        grid=(b,),
        in_specs=[
            pl.BlockSpec((1, s, LANES), lambda i: (i, 0, COL_BA // LANES)),
            pl.BlockSpec((1, LANES), lambda i: (0, 0)),
            pl.BlockSpec((1, LANES), lambda i: (0, 0)),
        ],
        out_specs=[
            pl.BlockSpec((1, s, LANES), lambda i: (i, 0, 0)),
            pl.BlockSpec((1, GP_GROUP, s), lambda i: (i, 0, 0)),
        ],
        out_shape=[
            jax.ShapeDtypeStruct((b, s, LANES), F32),
            jax.ShapeDtypeStruct((b, GP_GROUP, s), F32),
        ],
        compiler_params=pltpu.CompilerParams(
            dimension_semantics=("arbitrary",), vmem_limit_bytes=VMEM_LIMIT),
        name="gate_prep",
    )(p3, alog_row, dtb_row)


RG_GROUP = SUBLANES
RG_SLAB_PAD = 8
RG_TB = 32


def _rglru_kernel(rx_ref, ry_ref, cw_ref, cb_ref, w_ref, bias_ref, lam_ref, o_ref, u_s, h_s, pad_s,
                  ab0_s, ab1_s):
    s = rx_ref.shape[1]
    pitch = s + RG_SLAB_PAD
    nb = s // RG_TB
    cw_half = 0.5 * cw_ref[...]
    cb_half = 0.5 * cb_ref[...]
    _zero_halo(pad_s, s)

    def stage(b, carry):
        x = rx_ref[b].astype(F32)
        u_s[pl.ds(pl.multiple_of(b * pitch, SUBLANES), s), :] = _dwconv(pad_s, x, cw_half) + cb_half
        return carry

    lax.fori_loop(0, RG_GROUP, stage, 0)

    c2 = (-0.5 * RG_C * LOG2_E) * _softplus(-lam_ref[...])
    bias_half = 0.5 * bias_ref[...]

    def gather(ref, t0):
        return [ref[pl.ds(t0 + j, RG_GROUP, stride=pitch), :] for j in range(RG_TB)]

    def block_start(i, d):
        return (i if d == 0 else nb - 1 - i) * RG_TB

    def coefficients(i, ab_ref):
        i = jnp.minimum(i, nb - 1)
        for d in range(2):
            hu = jnp.concatenate(gather(u_s, block_start(i, d)), axis=0)
            gts = _dot(hu.astype(BF16), w_ref[:, d * 2 * LANES:(d + 1) * 2 * LANES])
            gts = gts + bias_half[:, d * 2 * LANES:(d + 1) * 2 * LANES]
            t_r = jnp.tanh(gts[:, :LANES])
            t_i = jnp.tanh(gts[:, LANES:])
            a = jnp.exp2(c2[d:d + 1] * t_r + c2[d:d + 1])
            om = 1.0 - a * a
            root = om * lax.rsqrt(jnp.maximum(om, TINY))
            ab_ref[d, 0] = a
            ab_ref[d, 1] = root * ((t_i + 1.0) * hu)

    def scan(i, hs, ab_ref, accumulate):
        out = []
        for d in range(2):
            t0 = block_start(i, d)
            h = hs[d]
            order = range(RG_TB) if d == 0 else range(RG_TB - 1, -1, -1)
            for j in order:
                rows = pl.ds(j * RG_GROUP, RG_GROUP)
                h = ab_ref[d, 0, rows, :] * h + ab_ref[d, 1, rows, :]
                dst = pl.ds(t0 + j, RG_GROUP, stride=pitch)
                if accumulate:
                    h_s[dst, :] = h_s[dst, :] + h
                else:
                    h_s[dst, :] = h
            out.append(h)
        return tuple(out)

    def make_body(accumulate):
        def body(k, hs):
            coefficients(2 * k + 1, ab1_s)
            hs = scan(2 * k, hs, ab0_s, accumulate)
            coefficients(2 * k + 2, ab0_s)
            return scan(2 * k + 1, hs, ab1_s, accumulate)
        return body

    zero = jnp.zeros((RG_GROUP, LANES), F32)
    coefficients(0, ab0_s)
    hs = lax.fori_loop(0, nb // 4, make_body(False), (zero, zero))
    lax.fori_loop(nb // 4, nb // 2, make_body(True), hs)

    def finish(b, carry):
        hsum = h_s[pl.ds(pl.multiple_of(b * pitch, SUBLANES), s), :]
        y = hsum * jax.nn.gelu(ry_ref[b].astype(F32))
        o_ref[b] = y.astype(o_ref.dtype)
        return carry

    lax.fori_loop(0, RG_GROUP, finish, 0)


def _rglru(p3, cw, cb, w, bias, lam):
    b, s, _ = p3.shape
    pitch = s + RG_SLAB_PAD
    blk = (RG_GROUP, s, LANES)
    return pl.pallas_call(
        _rglru_kernel,
        grid=(b // RG_GROUP, RNN_BLOCKS),
        in_specs=[
            pl.BlockSpec(blk, lambda g, n: (g, 0, COL_RX // LANES + n)),
            pl.BlockSpec(blk, lambda g, n: (g, 0, COL_RY // LANES + n)),
            pl.BlockSpec((CONV_K, LANES), lambda g, n: (0, n)),
            pl.BlockSpec((1, LANES), lambda g, n: (0, n)),
            pl.BlockSpec((None, RNN_BLOCK, 4 * RNN_BLOCK), lambda g, n: (n, 0, 0)),
            pl.BlockSpec((None, 1, 4 * RNN_BLOCK), lambda g, n: (n, 0, 0)),
            pl.BlockSpec((2, LANES), lambda g, n: (0, n)),
        ],
        out_specs=pl.BlockSpec(blk, lambda g, n: (g, 0, n)),
        out_shape=jax.ShapeDtypeStruct((b, s, D_RNN), BF16),
        scratch_shapes=[
            pltpu.VMEM((RG_GROUP * pitch, LANES), F32),
            pltpu.VMEM((RG_GROUP * pitch, LANES), F32),
            pltpu.VMEM((s + 2 * CONV_HALO, LANES), F32),
            pltpu.VMEM((2, 2, RG_TB * RG_GROUP, LANES), F32),
            pltpu.VMEM((2, 2, RG_TB * RG_GROUP, LANES), F32),
        ],
        compiler_params=pltpu.CompilerParams(
            dimension_semantics=("arbitrary", "arbitrary"), vmem_limit_bytes=VMEM_LIMIT),
        name="rglru",
    )(p3, p3, cw, cb, w, bias, lam)


DN_HB = 2
DN_G = 4
DN_TR = 256


def _deltanet_kernel(q_ref, k_ref, v_ref, z_ref, gp_ref, gt_ref, cwq_ref, cwk_ref, cwv_ref, ng_ref,
                     o_ref, qn_s, kn_s, vb_s, kbe_s, qg_s, kd_s, gpr_s, w_s, u_s, att_s, o_s, pad_s):
    s = q_ref.shape[1]
    nc = s // CHUNK
    head0 = pl.program_id(1) * DN_HB
    ones = jnp.ones((LANES, LANES), BF16)
    for idx in range(3):
        _zero_halo(pad_s.at[idx], s)

    for hh in range(DN_HB):
        lanes = slice(hh * LANES, (hh + 1) * LANES)
        gate_shift = (LANES - (head0 + hh)) % LANES
        for idx, ref in enumerate((q_ref, k_ref, v_ref)):
            pad_s[idx, CONV_HALO:CONV_HALO + s, :] = ref[0, :, lanes].astype(F32)
        cw_half = [0.5 * cw_ref[:, lanes] for cw_ref in (cwq_ref, cwk_ref, cwv_ref)]

        def seq_tile(i, carry):
            r0 = pl.multiple_of(i * DN_TR, DN_TR)
            rows = pl.ds(r0, DN_TR)

            def conv_silu(idx):
                cw = cw_half[idx]
                hy = cw[0:1] * pad_s[idx, pl.ds(r0 + CONV_HALO - 2, DN_TR), :]
                for j in range(1, CONV_K):
                    hy = hy + cw[j:j + 1] * pad_s[idx, pl.ds(r0 + CONV_HALO - 2 + j, DN_TR), :]
                return hy * jnp.tanh(hy) + hy

            q = conv_silu(0)
            k = conv_silu(1)
            v = conv_silu(2)
            qn = q * lax.rsqrt(_lane_sum(q * q, ones) + EPS) * (DN_DK ** -0.5)
            kn = k * lax.rsqrt(_lane_sum(k * k, ones) + EPS)
            qn_s[hh, rows, :] = qn.astype(BF16)
            kn_s[hh, rows, :] = kn.astype(BF16)
            gp = pltpu.roll(gp_ref[0, rows, :], gate_shift, axis=1)
            gpr_s[hh, rows, :] = gp

            def col(quantity, d):
                c = quantity * GP_GROUP + d * DN_HEADS
                return jnp.broadcast_to(gp[:, c:c + 1], (DN_TR, LANES))

            for d in range(2):
                sid = hh * 2 + d
                vb_s[sid, rows, :] = (v * col(GP_BETA, d)).astype(BF16)
                kbe_s[sid, rows, :] = (kn * col(GP_BEGC, d)).astype(BF16)
                qg_s[sid, rows, :] = (qn * col(GP_EGC, d)).astype(BF16)
                kd_s[sid, rows, :] = (kn * col(GP_EGR, d)).astype(BF16)
            return carry

        lax.fori_loop(0, s // DN_TR, seq_tile, 0)

    ii = lax.broadcasted_iota(jnp.int32, (CHUNK, CHUNK), 0)
    jj = lax.broadcasted_iota(jnp.int32, (CHUNK, CHUNK), 1)
    eye = (ii == jj).astype(F32)
    incl = ((ii >= jj), (ii <= jj))
    strict = ((ii > jj), (ii < jj))

    def prep_group(it):
        chains = []
        for hh in range(DN_HB):
            for g in range(DN_G):
                for d in range(2):
                    c = it * DN_G + g if d == 0 else nc - 1 - (it * DN_G + g)
                    chains.append((hh, d, c, pl.multiple_of(c * CHUNK, CHUNK)))
        xp, tinv = [], []
        for hh, d, c, r0 in chains:
            sid = hh * 2 + d
            knc = kn_s[hh, pl.ds(r0, CHUNK), :]
            kk = _dot_nt(knc, knc)
            qk = _dot_nt(qn_s[hh, pl.ds(r0, CHUNK), :], knc)

            def gcol(quantity):
                lane = quantity * GP_GROUP + d * DN_HEADS
                return jnp.broadcast_to(gpr_s[hh, pl.ds(r0, CHUNK), lane:lane + 1], (CHUNK, CHUNK))

            grow = gt_ref[0, pl.ds(d * DN_HEADS + head0 + hh, 1), pl.ds(c, 1), :]
            decay = jnp.where(incl[d], jnp.exp(gcol(GP_GC) - grow[0]), 0.0)
            att_s[sid, pl.ds(r0, CHUNK), :] = (qk * decay).astype(BF16)
            x = jnp.where(strict[d], -(gcol(GP_BETA) * kk * decay), 0.0)
            xp.append(x)
            tinv.append(eye + x)
        for _ in range(5):
            xpb = [x.astype(BF16) for x in xp]
            xp = [_dot(x, x) for x in xpb]
            tinv = [t + _dot(t.astype(BF16), x.astype(BF16)) for t, x in zip(tinv, xp)]
        for n, (hh, d, c, r0) in enumerate(chains):
            sid = hh * 2 + d
            rhs = jnp.concatenate([vb_s[sid, pl.ds(r0, CHUNK), :], kbe_s[sid, pl.ds(r0, CHUNK), :]], axis=1)
            uw = _dot(tinv[n].astype(BF16), rhs)
            u_s[sid, pl.ds(r0, CHUNK), :] = uw[:, :DN_DV]
            w_s[sid, pl.ds(r0, CHUNK), :] = uw[:, DN_DV:].astype(BF16)

    streams = [(hh, d) for hh in range(DN_HB) for d in range(2)]

    def chain_step(i, states):
        r0s = [pl.multiple_of((i if d == 0 else nc - 1 - i) * CHUNK, CHUNK) for _, d in streams]
        res = []
        for (hh, d), r0, st in zip(streams, r0s, states):
            sid = hh * 2 + d
            wq = jnp.concatenate([w_s[sid, pl.ds(r0, CHUNK), :], qg_s[sid, pl.ds(r0, CHUNK), :]], axis=0)
            res.append(_dot(wq, st.astype(BF16)))
        new_states = []
        for (hh, d), r0, st, rs in zip(streams, r0s, states, res):
            sid = hh * 2 + d
            v_new = (u_s[sid, pl.ds(r0, CHUNK), :] - rs[:CHUNK]).astype(BF16)
            o_s[sid, pl.ds(r0, CHUNK), :] = rs[CHUNK:] + _dot(att_s[sid, pl.ds(r0, CHUNK), :], v_new)
            egt_lane = GP_EGT * GP_GROUP + d * DN_HEADS
            egt = gpr_s[hh, pl.ds(r0, 1), egt_lane:egt_lane + 1]
            new_states.append(st * egt + _dot_tn(kd_s[sid, pl.ds(r0, CHUNK), :], v_new))
        return tuple(new_states)

    def chain_group(it, states):
        for g in range(DN_G):
            states = chain_step(it * DN_G + g, states)
        return states

    def pipelined(it, states):
        states = chain_group(it, states)
        prep_group(it + 1)
        return states

    ngroups = nc // DN_G
    zero = jnp.zeros((DN_DK, DN_DV), F32)
    prep_group(0)
    states = lax.fori_loop(0, ngroups - 1, pipelined, tuple(zero for _ in streams))
    chain_group(ngroups - 1, states)

    for hh in range(DN_HB):
        lanes = slice(hh * LANES, (hh + 1) * LANES)

        def out_tile(i, carry):
            rows = pl.ds(pl.multiple_of(i * DN_TR, DN_TR), DN_TR)
            o = o_s[hh * 2, rows, :] + o_s[hh * 2 + 1, rows, :]
            var = _lane_sum(o * o, ones) * (1.0 / DN_DV)
            hz = 0.5 * z_ref[0, rows, lanes].astype(F32)
            y = o * lax.rsqrt(var + EPS) * ng_ref[...] * (hz * jnp.tanh(hz) + hz)
            o_ref[0, rows, lanes] = y.astype(o_ref.dtype)
            return carry

        lax.fori_loop(0, s // DN_TR, out_tile, 0)


def _deltanet(p3, gp, gt4, cw, ng):
    b, s, _ = p3.shape
    width = DN_HB * LANES
    blk = (1, s, width)
    hblocks = DN_HEADS // DN_HB
    qkv0 = COL_QKV // width
    ns = 2 * DN_HB
    seq_bf16 = lambda: pltpu.VMEM((ns, s, LANES), BF16)
    seq_f32 = lambda: pltpu.VMEM((ns, s, LANES), F32)
    return pl.pallas_call(
        _deltanet_kernel,
        grid=(b, hblocks),
        in_specs=[
            pl.BlockSpec(blk, lambda i, h: (i, 0, qkv0 + h)),
            pl.BlockSpec(blk, lambda i, h: (i, 0, qkv0 + hblocks + h)),
            pl.BlockSpec(blk, lambda i, h: (i, 0, qkv0 + 2 * hblocks + h)),
            pl.BlockSpec(blk, lambda i, h: (i, 0, COL_Z // width + h)),
            pl.BlockSpec((1, s, LANES), lambda i, h: (i, 0, 0)),
            pl.BlockSpec((1, GP_GROUP, s // CHUNK, CHUNK), lambda i, h: (i, 0, 0, 0)),
            pl.BlockSpec((CONV_K, width), lambda i, h: (0, h)),
            pl.BlockSpec((CONV_K, width), lambda i, h: (0, hblocks + h)),
            pl.BlockSpec((CONV_K, width), lambda i, h: (0, 2 * hblocks + h)),
            pl.BlockSpec((1, LANES), lambda i, h: (0, 0)),
        ],
        out_specs=pl.BlockSpec(blk, lambda i, h: (i, 0, h)),
        out_shape=jax.ShapeDtypeStruct((b, s, DN_V), BF16),
        scratch_shapes=[
            pltpu.VMEM((DN_HB, s, LANES), BF16),
            pltpu.VMEM((DN_HB, s, LANES), BF16),
            seq_bf16(),
            seq_bf16(),
            seq_bf16(),
            seq_bf16(),
            pltpu.VMEM((DN_HB, s, LANES), F32),
            seq_bf16(),
            seq_f32(),
            pltpu.VMEM((ns, s, CHUNK), BF16),
            seq_f32(),
            pltpu.VMEM((3, s + 2 * CONV_HALO, LANES), F32),
        ],
        compiler_params=pltpu.CompilerParams(
            dimension_semantics=("arbitrary", "arbitrary"), vmem_limit_bytes=VMEM_LIMIT),
        name="deltanet",
    )(p3, p3, p3, p3, gp, gt4, cw, cw, cw, ng)


FF_CHUNK = 256


def _merge_ffn_kernel(x_ref, yr_ref, yd_ref, gr_ref, gd_ref, wr_ref, wd_ref, wo_ref, fn_ref, wgu_ref,
                      wdn_ref, o_ref):
    y_rnn = _dot(yr_ref[...], wr_ref[...])
    y_dn = _dot(yd_ref[...], wd_ref[...])
    merged = (_sigmoid(gr_ref[...].astype(F32)) * y_rnn + _sigmoid(gd_ref[...].astype(F32)) * y_dn)
    x1 = x_ref[...] + _dot(merged.astype(BF16), wo_ref[...])
    ms = jnp.mean(x1 * x1, axis=-1, keepdims=True)
    h = (x1 * lax.rsqrt(ms + EPS) * fn_ref[...]).astype(BF16)
    acc = x1
    for c0 in range(0, D_FF, FF_CHUNK):
        gt = _dot(h, wgu_ref[:, c0:c0 + FF_CHUNK])
        up = _dot(h, wgu_ref[:, D_FF + c0:D_FF + c0 + FF_CHUNK])
        act = (gt * _sigmoid(gt) * up).astype(BF16)
        acc = acc + _dot(act, wdn_ref[c0:c0 + FF_CHUNK, :])
    o_ref[...] = acc


def _merge_ffn(x2d, y_rnn, y_dn, p2d, w_rnn, w_dn, w_out, ffn_gain, w_gu, w_down, *, tm):
    t, d = x2d.shape
    full = lambda a: pl.BlockSpec(a.shape, lambda i: (0,) * a.ndim)
    return pl.pallas_call(
        _merge_ffn_kernel,
        grid=(t // tm,),
        in_specs=[
            pl.BlockSpec((tm, d), lambda i: (i, 0)),
            pl.BlockSpec((tm, D_RNN), lambda i: (i, 0)),
            pl.BlockSpec((tm, DN_V), lambda i: (i, 0)),
            pl.BlockSpec((tm, d), lambda i: (i, COL_GATES // D_MODEL)),
            pl.BlockSpec((tm, d), lambda i: (i, COL_GATES // D_MODEL + 1)),
            full(w_rnn), full(w_dn), full(w_out), full(ffn_gain), full(w_gu), full(w_down),
        ],
        out_specs=pl.BlockSpec((tm, d), lambda i: (i, 0)),
        out_shape=jax.ShapeDtypeStruct((t, d), F32),
        compiler_params=pltpu.CompilerParams(
            dimension_semantics=("arbitrary",), vmem_limit_bytes=VMEM_LIMIT),
        name="merge_ffn",
    )(x2d, y_rnn, y_dn, p2d, p2d, w_rnn, w_dn, w_out, ffn_gain, w_gu, w_down)


def _final_norm_kernel(x_ref, g_ref, o_ref):
    x = x_ref[...]
    ms = jnp.mean(x * x, axis=-1, keepdims=True)
    o_ref[...] = x * lax.rsqrt(ms + EPS) * g_ref[...]


def _final_norm(x2d, gain, *, tm):
    t, d = x2d.shape
    return pl.pallas_call(
        _final_norm_kernel,
        grid=(t // tm,),
        in_specs=[pl.BlockSpec((tm, d), lambda i: (i, 0)), pl.BlockSpec((1, d), lambda i: (0, 0))],
        out_specs=pl.BlockSpec((tm, d), lambda i: (i, 0)),
        out_shape=jax.ShapeDtypeStruct((t, d), F32),
        compiler_params=pltpu.CompilerParams(dimension_semantics=("arbitrary",)),
        name="final_norm",
    )(x2d, gain)


def _pack_w_in(w_in_l):
    split_ba = COL_Z + DN_V
    pad = jnp.zeros((D_MODEL, LANES - N_BA), w_in_l.dtype)
    return jnp.concatenate(
        [w_in_l[:, :split_ba], w_in_l[:, split_ba + N_BA:], w_in_l[:, split_ba:split_ba + N_BA], pad],
        axis=1).astype(BF16)


def _pack_rg_gates(wa, wx, ba, bx):
    w = jnp.concatenate([wa[0], wx[0], wa[1], wx[1]], axis=-1).astype(BF16)
    blocks = lambda v: v.reshape(RNN_BLOCKS, 1, RNN_BLOCK)
    bias = jnp.concatenate([blocks(ba[0]), blocks(bx[0]), blocks(ba[1]), blocks(bx[1])], axis=-1)
    return w, bias.astype(F32)


def _gate_rows(a_log, dt_bias):
    pad = jnp.zeros((LANES - GP_GROUP,), F32)
    row = lambda v: jnp.concatenate([v.reshape(-1).astype(F32), pad]).reshape(1, LANES)
    return row(a_log), row(dt_bias)


def kernel(x, mix_norm, w_in, rg_conv_w, rg_conv_b, rg_wa, rg_ba, rg_wx, rg_bx, rg_lambda, w_rnn_proj,
           dn_conv_w, dn_a_log, dn_dt_bias, dn_norm, w_dn_proj, w_out, ffn_norm, w_gate_up, w_down,
           final_norm):
    b, s, d = x.shape
    depth = w_in.shape[0]
    t = b * s
    assert d == D_MODEL and b % RG_GROUP == 0 and s % (4 * RG_TB) == 0 and s % (DN_G * CHUNK) == 0
    tm = 512 if t % 512 == 0 else 256
    x2d = x.reshape(t, d)
    for l in range(depth):
        p2d = _in_proj(x2d, mix_norm[l].reshape(1, d), _pack_w_in(w_in[l]), tm=tm)
        p3 = p2d.reshape(b, s, N_P)
        alog_row, dtb_row = _gate_rows(dn_a_log[l], dn_dt_bias[l])
        gp, gt = _gate_prep(p3, alog_row, dtb_row)
        gt4 = gt.reshape(b, GP_GROUP, s // CHUNK, CHUNK)
        rg_w, rg_bias = _pack_rg_gates(rg_wa[l], rg_wx[l], rg_ba[l], rg_bx[l])
        y_rnn = _rglru(p3, rg_conv_w[l], rg_conv_b[l].reshape(1, D_RNN), rg_w, rg_bias, rg_lambda[l])
        y_dn = _deltanet(p3, gp, gt4, dn_conv_w[l], dn_norm[l].reshape(1, DN_DV))
        x2d = _merge_ffn(
            x2d, y_rnn.reshape(t, D_RNN), y_dn.reshape(t, DN_V), p2d,
            w_rnn_proj[l].astype(BF16), w_dn_proj[l].astype(BF16), w_out[l].astype(BF16),
            ffn_norm[l].reshape(1, d), w_gate_up[l].astype(BF16), w_down[l].astype(BF16), tm=tm)
    out = _final_norm(x2d, final_norm.reshape(1, d), tm=tm)
    return out.reshape(b, s, d)
```

```python
import functools

import jax
import jax.numpy as jnp
from jax import lax
from jax.experimental import pallas as pl
from jax.experimental.pallas import tpu as pltpu

F32 = jnp.float32
BF16 = jnp.bfloat16

D_MODEL = 1024
D_RNN = 1536
RNN_BLOCK = 128
RNN_BLOCKS = D_RNN // RNN_BLOCK
RG_C = 8.0
CONV_K = 4
DN_HEADS = 8
DN_DK = 128
DN_DV = 128
DN_QK = DN_HEADS * DN_DK
DN_V = DN_HEADS * DN_DV
DN_QKV = 2 * DN_QK + DN_V
CHUNK = 64
D_FF = 2816
EPS = 1e-6
N_BA = 4 * DN_HEADS
LOG2_E = 1.4426950408889634
TINY = 1e-30

LANES = 128
SUBLANES = 8

COL_RX = 0
COL_RY = D_RNN
COL_QKV = 2 * D_RNN
COL_Z = COL_QKV + DN_QKV
COL_GATES = COL_Z + DN_V
COL_BA = COL_GATES + 2 * D_MODEL
N_P = COL_BA + LANES

VMEM_LIMIT = 56 * 1024 * 1024


def _sigmoid(x):
    return 1.0 / (1.0 + jnp.exp(-x))


def _softplus(x):
    return jnp.maximum(x, 0.0) + jnp.log(1.0 + jnp.exp(-jnp.abs(x)))


def _dot(a, b):
    return jnp.dot(a, b, preferred_element_type=F32)


def _dot_nt(a, b):
    return lax.dot_general(a, b, (((1,), (1,)), ((), ())), preferred_element_type=F32)


def _dot_tn(a, b):
    return lax.dot_general(a, b, (((0,), (0,)), ((), ())), preferred_element_type=F32)


def _lane_sum(x, ones_bf16):
    return _dot(x.astype(BF16), ones_bf16)


CONV_HALO = SUBLANES


def _zero_halo(pad_ref, s):
    zeros = jnp.zeros((CONV_HALO, pad_ref.shape[1]), F32)
    pad_ref[0:CONV_HALO, :] = zeros
    pad_ref[CONV_HALO + s:2 * CONV_HALO + s, :] = zeros


def _dwconv(pad_ref, x, cw):
    s = x.shape[0]
    pad_ref[CONV_HALO:CONV_HALO + s, :] = x
    y = cw[0:1] * pad_ref[CONV_HALO - 2:CONV_HALO - 2 + s, :]
    for j in range(1, CONV_K):
        y = y + cw[j:j + 1] * pad_ref[CONV_HALO - 2 + j:CONV_HALO - 2 + j + s, :]
    return y


def _in_proj_kernel(x_ref, g_ref, w_ref, o_ref, *, col_chunk):
    x = x_ref[...]
    ms = jnp.mean(x * x, axis=-1, keepdims=True)
    h = (x * lax.rsqrt(ms + EPS) * g_ref[...]).astype(BF16)
    n = o_ref.shape[1]
    for c0 in range(0, n, col_chunk):
        cn = min(col_chunk, n - c0)
        o_ref[:, c0:c0 + cn] = _dot(h, w_ref[:, c0:c0 + cn]).astype(o_ref.dtype)


def _in_proj(x2d, gain, w, *, tm):
    t, d = x2d.shape
    n = w.shape[1]
    return pl.pallas_call(
        functools.partial(_in_proj_kernel, col_chunk=1024),
        grid=(t // tm,),
        in_specs=[
            pl.BlockSpec((tm, d), lambda i: (i, 0)),
            pl.BlockSpec((1, d), lambda i: (0, 0)),
            pl.BlockSpec((d, n), lambda i: (0, 0)),
        ],
        out_specs=pl.BlockSpec((tm, n), lambda i: (i, 0)),
        out_shape=jax.ShapeDtypeStruct((t, n), BF16),
        compiler_params=pltpu.CompilerParams(
            dimension_semantics=("arbitrary",), vmem_limit_bytes=VMEM_LIMIT),
        name="in_proj",
    )(x2d, gain, w)


GP_BETA, GP_GC, GP_EGC, GP_BEGC, GP_EGR, GP_EGT = range(6)
GP_GROUP = 2 * DN_HEADS
GT_ROWS = (GP_BETA, GP_GC, GP_BEGC)
GT_BETA, GT_GC, GT_BEGC = range(3)
DN_HB = 2


def _gate_prep_kernel(ba_ref, alog_ref, dtb_ref, gp_ref, gt_ref):
    s = ba_ref.shape[1]
    x = ba_ref[0].astype(F32)
    lane = lax.broadcasted_iota(jnp.int32, (s, LANES), 1)
    row = lax.broadcasted_iota(jnp.int32, (s, LANES), 0)
    ic = row & (CHUNK - 1)
    beta = _sigmoid(x)
    xs = pltpu.roll(x, LANES - GP_GROUP, axis=1)
    g = -jnp.exp(alog_ref[...]) * _softplus(xs + dtb_ref[...])
    fsum = g
    rsum = g
    sh = 1
    while sh < CHUNK:
        fsum = fsum + jnp.where(ic >= sh, pltpu.roll(fsum, sh, axis=0), 0.0)
        rsum = rsum + jnp.where(ic < CHUNK - sh, pltpu.roll(rsum, s - sh, axis=0), 0.0)
        sh *= 2
    is_fwd = (lane & (GP_GROUP - 1)) < DN_HEADS
    gc = jnp.where(is_fwd, fsum, rsum)
    gr = jnp.where(is_fwd, rsum, fsum) - g
    gtot = fsum + rsum - g
    egc = jnp.exp(gc)
    parts = (beta, gc, egc, beta * egc, jnp.exp(gr), jnp.exp(gtot))
    out = parts[-1]
    out = pltpu.roll(out, (len(parts) - 1) * GP_GROUP, axis=1)
    for qi in range(len(parts) - 2, -1, -1):
        placed = parts[qi] if qi == 0 else pltpu.roll(parts[qi], qi * GP_GROUP, axis=1)
        out = jnp.where(lane < (qi + 1) * GP_GROUP, placed, out)
    for p in range(DN_HEADS // DN_HB):
        gp_ref[0, p] = out if p == 0 else pltpu.roll(out, LANES - p * DN_HB, axis=1)
    out_t = out.T
    for n, quantity in enumerate(GT_ROWS):
        gt_ref[0, n] = out_t[quantity * GP_GROUP:(quantity + 1) * GP_GROUP, :]


def _gate_prep(p3, alog_row, dtb_row):
    b, s, _ = p3.shape
    return pl.pallas_call(
        _gate_prep_kernel,
        grid=(b,),
        in_specs=[
            pl.BlockSpec((1, s, LANES), lambda i: (i, 0, COL_BA // LANES)),
            pl.BlockSpec((1, LANES), lambda i: (0, 0)),
            pl.BlockSpec((1, LANES), lambda i: (0, 0)),
        ],
        out_specs=[
            pl.BlockSpec((1, DN_HEADS // DN_HB, s, LANES), lambda i: (i, 0, 0, 0)),
            pl.BlockSpec((1, len(GT_ROWS), GP_GROUP, s), lambda i: (i, 0, 0, 0)),
        ],
        out_shape=[
            jax.ShapeDtypeStruct((b, DN_HEADS // DN_HB, s, LANES), F32),
            jax.ShapeDtypeStruct((b, len(GT_ROWS), GP_GROUP, s), F32),
        ],
        compiler_params=pltpu.CompilerParams(
            dimension_semantics=("arbitrary",), vmem_limit_bytes=VMEM_LIMIT),
        name="gate_prep",
    )(p3, alog_row, dtb_row)


RG_GROUP = SUBLANES
RG_SLAB_PAD = 8
RG_TB = 32


def _rglru_kernel(rx_ref, ry_ref, cw_ref, cb_ref, w_ref, bias_ref, lam_ref, o_ref, u_s, h_s, pad_s,
                  ab0_s, ab1_s):
    s = rx_ref.shape[1]
    pitch = s + RG_SLAB_PAD
    nb = s // RG_TB
    cw_half = 0.5 * cw_ref[...]
    cb_half = 0.5 * cb_ref[...]
    _zero_halo(pad_s, s)

    def stage(b, carry):
        x = rx_ref[b].astype(F32)
        u_s[pl.ds(pl.multiple_of(b * pitch, SUBLANES), s), :] = _dwconv(pad_s, x, cw_half) + cb_half
        return carry

    lax.fori_loop(0, RG_GROUP, stage, 0)

    c2 = (-0.5 * RG_C * LOG2_E) * _softplus(-lam_ref[...])
    bias_half = 0.5 * bias_ref[...]

    def gather(ref, t0):
        return [ref[pl.ds(t0 + j, RG_GROUP, stride=pitch), :] for j in range(RG_TB)]

    def block_start(i, d):
        return (i if d == 0 else nb - 1 - i) * RG_TB

    def coefficients(i, ab_ref):
        i = jnp.minimum(i, nb - 1)
        for d in range(2):
            hu = jnp.concatenate(gather(u_s, block_start(i, d)), axis=0)
            gts = _dot(hu.astype(BF16), w_ref[:, d * 2 * LANES:(d + 1) * 2 * LANES])
            gts = gts + bias_half[:, d * 2 * LANES:(d + 1) * 2 * LANES]
            t_r = jnp.tanh(gts[:, :LANES])
            t_i = jnp.tanh(gts[:, LANES:])
            a = jnp.exp2(c2[d:d + 1] * t_r + c2[d:d + 1])
            om = 1.0 - a * a
            root = om * lax.rsqrt(jnp.maximum(om, TINY))
            ab_ref[d, 0] = a
            ab_ref[d, 1] = root * ((t_i + 1.0) * hu)

    def scan(i, hs, ab_ref, accumulate):
        out = []
        for d in range(2):
            t0 = block_start(i, d)
            h = hs[d]
            order = range(RG_TB) if d == 0 else range(RG_TB - 1, -1, -1)
            for j in order:
                rows = pl.ds(j * RG_GROUP, RG_GROUP)
                h = ab_ref[d, 0, rows, :] * h + ab_ref[d, 1, rows, :]
                dst = pl.ds(t0 + j, RG_GROUP, stride=pitch)
                if accumulate:
                    h_s[dst, :] = h_s[dst, :] + h
                else:
                    h_s[dst, :] = h
            out.append(h)
        return tuple(out)

    def make_body(accumulate):
        def body(k, hs):
            coefficients(2 * k + 1, ab1_s)
            hs = scan(2 * k, hs, ab0_s, accumulate)
            coefficients(2 * k + 2, ab0_s)
            return scan(2 * k + 1, hs, ab1_s, accumulate)
        return body

    zero = jnp.zeros((RG_GROUP, LANES), F32)
    coefficients(0, ab0_s)
    hs = lax.fori_loop(0, nb // 4, make_body(False), (zero, zero))
    lax.fori_loop(nb // 4, nb // 2, make_body(True), hs)

    def finish(b, carry):
        hsum = h_s[pl.ds(pl.multiple_of(b * pitch, SUBLANES), s), :]
        y = hsum * jax.nn.gelu(ry_ref[b].astype(F32))
        o_ref[b] = y.astype(o_ref.dtype)
        return carry

    lax.fori_loop(0, RG_GROUP, finish, 0)


def _rglru(p3, cw, cb, w, bias, lam):
    b, s, _ = p3.shape
    pitch = s + RG_SLAB_PAD
    blk = (RG_GROUP, s, LANES)
    return pl.pallas_call(
        _rglru_kernel,
        grid=(b // RG_GROUP, RNN_BLOCKS),
        in_specs=[
            pl.BlockSpec(blk, lambda g, n: (g, 0, COL_RX // LANES + n)),
            pl.BlockSpec(blk, lambda g, n: (g, 0, COL_RY // LANES + n)),
            pl.BlockSpec((CONV_K, LANES), lambda g, n: (0, n)),
            pl.BlockSpec((1, LANES), lambda g, n: (0, n)),
            pl.BlockSpec((None, RNN_BLOCK, 4 * RNN_BLOCK), lambda g, n: (n, 0, 0)),
            pl.BlockSpec((None, 1, 4 * RNN_BLOCK), lambda g, n: (n, 0, 0)),
            pl.BlockSpec((2, LANES), lambda g, n: (0, n)),
        ],
        out_specs=pl.BlockSpec(blk, lambda g, n: (g, 0, n)),
        out_shape=jax.ShapeDtypeStruct((b, s, D_RNN), BF16),
        scratch_shapes=[
            pltpu.VMEM((RG_GROUP * pitch, LANES), F32),
            pltpu.VMEM((RG_GROUP * pitch, LANES), F32),
            pltpu.VMEM((s + 2 * CONV_HALO, LANES), F32),
            pltpu.VMEM((2, 2, RG_TB * RG_GROUP, LANES), F32),
            pltpu.VMEM((2, 2, RG_TB * RG_GROUP, LANES), F32),
        ],
        compiler_params=pltpu.CompilerParams(
            dimension_semantics=("arbitrary", "arbitrary"), vmem_limit_bytes=VMEM_LIMIT),
        name="rglru",
    )(p3, p3, cw, cb, w, bias, lam)


DN_G = 4
DN_TR = 256


def _deltanet_kernel(q_ref, k_ref, v_ref, z_ref, gp_ref, gt_ref, cwq_ref, cwk_ref, cwv_ref, ng_ref,
                     o_ref, qn_s, kn_s, v_s, qg_s, kd_s, w_s, u_s, att_s, o_s, pad_s):
    s = q_ref.shape[1]
    nc = s // CHUNK
    head0 = pl.program_id(1) * DN_HB
    ones = jnp.ones((LANES, LANES), BF16)
    for idx in range(3):
        _zero_halo(pad_s.at[idx], s)

    def gate_lane(quantity, d, hh):
        return quantity * GP_GROUP + d * DN_HEADS + hh

    for hh in range(DN_HB):
        lanes = slice(hh * LANES, (hh + 1) * LANES)
        for idx, ref in enumerate((q_ref, k_ref, v_ref)):
            pad_s[idx, CONV_HALO:CONV_HALO + s, :] = ref[0, :, lanes].astype(F32)
        cw_half = [0.5 * cw_ref[:, lanes] for cw_ref in (cwq_ref, cwk_ref, cwv_ref)]

        def seq_tile(i, carry):
            r0 = pl.multiple_of(i * DN_TR, DN_TR)
            rows = pl.ds(r0, DN_TR)

            def conv_silu(idx):
                cw = cw_half[idx]
                hy = cw[0:1] * pad_s[idx, pl.ds(r0 + CONV_HALO - 2, DN_TR), :]
                for j in range(1, CONV_K):
                    hy = hy + cw[j:j + 1] * pad_s[idx, pl.ds(r0 + CONV_HALO - 2 + j, DN_TR), :]
                return hy * jnp.tanh(hy) + hy

            q = conv_silu(0)
            k = conv_silu(1)
            v = conv_silu(2)
            qn = q * lax.rsqrt(_lane_sum(q * q, ones) + EPS) * (DN_DK ** -0.5)
            kn = k * lax.rsqrt(_lane_sum(k * k, ones) + EPS)
            qn_s[hh, rows, :] = qn.astype(BF16)
            kn_s[hh, rows, :] = kn.astype(BF16)
            v_s[hh, rows, :] = v.astype(BF16)
            gp = gp_ref[0, rows, :]

            def col(quantity, d):
                c = gate_lane(quantity, d, hh)
                return jnp.broadcast_to(gp[:, c:c + 1], (DN_TR, LANES))

            for d in range(2):
                sid = hh * 2 + d
                qg_s[sid, rows, :] = (qn * col(GP_EGC, d)).astype(BF16)
                kd_s[sid, rows, :] = (kn * col(GP_EGR, d)).astype(BF16)
            return carry

        lax.fori_loop(0, s // DN_TR, seq_tile, 0)

    row = lax.broadcasted_iota(jnp.int32, (CHUNK, 2 * CHUNK), 0)
    lane = lax.broadcasted_iota(jnp.int32, (CHUNK, 2 * CHUNK), 1)
    is_f = lane < CHUNK
    ahead = jnp.where(is_f, row - lane, lane - CHUNK - row)
    incl = ahead >= 0
    strict = ahead > 0
    eye = (ahead == 0).astype(F32)
    keep_f = is_f.astype(F32).astype(BF16)
    keep_b = (1.0 - is_f.astype(F32)).astype(BF16)
    ztile = jnp.zeros((CHUNK, LANES), BF16)

    def block_diag(xb):
        return jnp.concatenate([xb * keep_f, xb * keep_b], axis=0)

    def diag2(f, b):
        return jnp.concatenate([jnp.concatenate([f, ztile], axis=1), jnp.concatenate([ztile, b], axis=1)], axis=0)

    def gate_row(quantity, hh, j):
        return gt_ref[0, quantity, pl.ds(head0 + hh, 1), pl.ds(j, 1), :][0]

    def prep_stages(it):
        pairs = []
        for hh in range(DN_HB):
            for g in range(DN_G):
                j = it * DN_G + g
                pairs.append((hh, j, pl.multiple_of(j * CHUNK, CHUNK), pl.multiple_of((nc - 1 - j) * CHUNK, CHUNK)))
        xb, tinv = [], []
        for hh, j, rf, rb in pairs:
            knf = kn_s[hh, pl.ds(rf, CHUNK), :]
            knb = kn_s[hh, pl.ds(rb, CHUNK), :]
            kcat = jnp.concatenate([knf, knb], axis=1)
            qcat = jnp.concatenate([qn_s[hh, pl.ds(rf, CHUNK), :], qn_s[hh, pl.ds(rb, CHUNK), :]], axis=1)
            kdiag = diag2(knf, knb)
            kk = _dot_nt(kcat, kdiag)
            qk = _dot_nt(qcat, kdiag)

            def gcol(quantity):
                lf = gate_lane(quantity, 0, hh)
                lb = gate_lane(quantity, 1, hh)
                f = jnp.broadcast_to(gp_ref[0, pl.ds(rf, CHUNK), lf:lf + 1], (CHUNK, 2 * CHUNK))
                b = jnp.broadcast_to(gp_ref[0, pl.ds(rb, CHUNK), lb:lb + 1], (CHUNK, 2 * CHUNK))
                return jnp.where(is_f, f, b)

            grow = gate_row(GT_GC, hh, j)
            decay = jnp.where(incl, jnp.exp(gcol(GP_GC) - grow), 0.0)
            att_s[hh, pl.ds(rf, CHUNK), :] = (qk * decay).astype(BF16)
            x = jnp.where(strict, -(gcol(GP_BETA) * kk * decay), 0.0)
            xb.append(x.astype(BF16))
            tinv.append(eye + x)
        diag = [block_diag(x) for x in xb]
        yield
        for _ in range(5):
            xb = [_dot(x, m).astype(BF16) for x, m in zip(xb, diag)]
            diag = [block_diag(x) for x in xb]
            tinv = [t + _dot(t.astype(BF16), m) for t, m in zip(tinv, diag)]
            yield
        for n, (hh, j, rf, rb) in enumerate(pairs):
            sf, sb = hh * 2, hh * 2 + 1
            t_u = (tinv[n] * gate_row(GT_BETA, hh, j)).astype(BF16)
            t_w = (tinv[n] * gate_row(GT_BEGC, hh, j)).astype(BF16)
            u = _dot(t_u, diag2(v_s[hh, pl.ds(rf, CHUNK), :], v_s[hh, pl.ds(rb, CHUNK), :]))
            w = _dot(t_w, diag2(kn_s[hh, pl.ds(rf, CHUNK), :], kn_s[hh, pl.ds(rb, CHUNK), :]))
            u_s[sf, pl.ds(rf, CHUNK), :] = u[:, :DN_DV]
            u_s[sb, pl.ds(rb, CHUNK), :] = u[:, DN_DV:]
            w_s[sf, pl.ds(rf, CHUNK), :] = w[:, :DN_DK].astype(BF16)
            w_s[sb, pl.ds(rb, CHUNK), :] = w[:, DN_DK:].astype(BF16)
        yield

    streams = [(hh, d) for hh in range(DN_HB) for d in range(2)]

    def chunk_rows(i):
        return [pl.multiple_of((i if d == 0 else nc - 1 - i) * CHUNK, CHUNK) for _, d in streams]

    def chain_read(i, states):
        res = []
        for (hh, d), r0, st in zip(streams, chunk_rows(i), states):
            sid = hh * 2 + d
            wq = jnp.concatenate([w_s[sid, pl.ds(r0, CHUNK), :], qg_s[sid, pl.ds(r0, CHUNK), :]], axis=0)
            res.append(_dot(wq, st.astype(BF16)))
        return res

    def chain_update(i, states, res):
        new_states = []
        for (hh, d), r0, st, rs in zip(streams, chunk_rows(i), states, res):
            sid = hh * 2 + d
            v_new = (u_s[sid, pl.ds(r0, CHUNK), :] - rs[:CHUNK]).astype(BF16)
            att = att_s[hh, pl.ds(pl.multiple_of(i * CHUNK, CHUNK), CHUNK), :]
            v_pad = jnp.concatenate([v_new, ztile] if d == 0 else [ztile, v_new], axis=0)
            o_s[sid, pl.ds(r0, CHUNK), :] = rs[CHUNK:] + _dot(att, v_pad)
            egt_lane = gate_lane(GP_EGT, d, hh)
            egt = gp_ref[0, pl.ds(r0, 1), egt_lane:egt_lane + 1]
            new_states.append(st * egt + _dot_tn(kd_s[sid, pl.ds(r0, CHUNK), :], v_new))
        return tuple(new_states)

    def chain_group(it, states, stages=None):
        def advance():
            if stages is not None:
                next(stages, None)
        for g in range(DN_G):
            res = chain_read(it * DN_G + g, states)
            advance()
            states = chain_update(it * DN_G + g, states, res)
            advance()
        if stages is not None:
            for _ in stages:
                pass
        return states

    def pipelined(it, states):
        return chain_group(it, states, prep_stages(it + 1))

    ngroups = nc // DN_G
    zero = jnp.zeros((DN_DK, DN_DV), F32)
    for _ in prep_stages(0):
        pass
    states = lax.fori_loop(0, ngroups - 1, pipelined, tuple(zero for _ in streams))
    chain_group(ngroups - 1, states)

    for hh in range(DN_HB):
        lanes = slice(hh * LANES, (hh + 1) * LANES)

        def out_tile(i, carry):
            rows = pl.ds(pl.multiple_of(i * DN_TR, DN_TR), DN_TR)
            o = o_s[hh * 2, rows, :] + o_s[hh * 2 + 1, rows, :]
            var = _lane_sum(o * o, ones) * (1.0 / DN_DV)
            hz = 0.5 * z_ref[0, rows, lanes].astype(F32)
            y = o * lax.rsqrt(var + EPS) * ng_ref[...] * (hz * jnp.tanh(hz) + hz)
            o_ref[0, rows, lanes] = y.astype(o_ref.dtype)
            return carry

        lax.fori_loop(0, s // DN_TR, out_tile, 0)


def _deltanet(p3, gp, gt4, cw, ng):
    b, s, _ = p3.shape
    width = DN_HB * LANES
    blk = (1, s, width)
    hblocks = DN_HEADS // DN_HB
    qkv0 = COL_QKV // width
    ns = 2 * DN_HB
    seq_bf16 = lambda: pltpu.VMEM((ns, s, LANES), BF16)
    seq_f32 = lambda: pltpu.VMEM((ns, s, LANES), F32)
    return pl.pallas_call(
        _deltanet_kernel,
        grid=(b, hblocks),
        in_specs=[
            pl.BlockSpec(blk, lambda i, h: (i, 0, qkv0 + h)),
            pl.BlockSpec(blk, lambda i, h: (i, 0, qkv0 + hblocks + h)),
            pl.BlockSpec(blk, lambda i, h: (i, 0, qkv0 + 2 * hblocks + h)),
            pl.BlockSpec(blk, lambda i, h: (i, 0, COL_Z // width + h)),
            pl.BlockSpec((1, None, s, LANES), lambda i, h: (i, h, 0, 0)),
            pl.BlockSpec((1, len(GT_ROWS), DN_HEADS, s // CHUNK, 2 * CHUNK), lambda i, h: (i, 0, 0, 0, 0)),
            pl.BlockSpec((CONV_K, width), lambda i, h: (0, h)),
            pl.BlockSpec((CONV_K, width), lambda i, h: (0, hblocks + h)),
            pl.BlockSpec((CONV_K, width), lambda i, h: (0, 2 * hblocks + h)),
            pl.BlockSpec((1, LANES), lambda i, h: (0, 0)),
        ],
        out_specs=pl.BlockSpec(blk, lambda i, h: (i, 0, h)),
        out_shape=jax.ShapeDtypeStruct((b, s, DN_V), BF16),
        scratch_shapes=[
            pltpu.VMEM((DN_HB, s, LANES), BF16),
            pltpu.VMEM((DN_HB, s, LANES), BF16),
            pltpu.VMEM((DN_HB, s, LANES), BF16),
            seq_bf16(),
            seq_bf16(),
            seq_bf16(),
            seq_f32(),
            pltpu.VMEM((DN_HB, s, 2 * CHUNK), BF16),
            seq_f32(),
            pltpu.VMEM((3, s + 2 * CONV_HALO, LANES), F32),
        ],
        compiler_params=pltpu.CompilerParams(
            dimension_semantics=("arbitrary", "arbitrary"), vmem_limit_bytes=VMEM_LIMIT),
        name="deltanet",
    )(p3, p3, p3, p3, gp, gt4, cw, cw, cw, ng)


FF_CHUNK = 256


def _merge_ffn_kernel(x_ref, yr_ref, yd_ref, gr_ref, gd_ref, wr_ref, wd_ref, wo_ref, fn_ref, wgu_ref,
                      wdn_ref, o_ref):
    y_rnn = _dot(yr_ref[...], wr_ref[...])
    y_dn = _dot(yd_ref[...], wd_ref[...])
    merged = (_sigmoid(gr_ref[...].astype(F32)) * y_rnn + _sigmoid(gd_ref[...].astype(F32)) * y_dn)
    x1 = x_ref[...] + _dot(merged.astype(BF16), wo_ref[...])
    ms = jnp.mean(x1 * x1, axis=-1, keepdims=True)
    h = (x1 * lax.rsqrt(ms + EPS) * fn_ref[...]).astype(BF16)
    acc = x1
    for c0 in range(0, D_FF, FF_CHUNK):
        gt = _dot(h, wgu_ref[:, c0:c0 + FF_CHUNK])
        up = _dot(h, wgu_ref[:, D_FF + c0:D_FF + c0 + FF_CHUNK])
        act = (gt * _sigmoid(gt) * up).astype(BF16)
        acc = acc + _dot(act, wdn_ref[c0:c0 + FF_CHUNK, :])
    o_ref[...] = acc


def _merge_ffn(x2d, y_rnn, y_dn, p2d, w_rnn, w_dn, w_out, ffn_gain, w_gu, w_down, *, tm):
    t, d = x2d.shape
    full = lambda a: pl.BlockSpec(a.shape, lambda i: (0,) * a.ndim)
    return pl.pallas_call(
        _merge_ffn_kernel,
        grid=(t // tm,),
        in_specs=[
            pl.BlockSpec((tm, d), lambda i: (i, 0)),
            pl.BlockSpec((tm, D_RNN), lambda i: (i, 0)),
            pl.BlockSpec((tm, DN_V), lambda i: (i, 0)),
            pl.BlockSpec((tm, d), lambda i: (i, COL_GATES // D_MODEL)),
            pl.BlockSpec((tm, d), lambda i: (i, COL_GATES // D_MODEL + 1)),
            full(w_rnn), full(w_dn), full(w_out), full(ffn_gain), full(w_gu), full(w_down),
        ],
        out_specs=pl.BlockSpec((tm, d), lambda i: (i, 0)),
        out_shape=jax.ShapeDtypeStruct((t, d), F32),
        compiler_params=pltpu.CompilerParams(
            dimension_semantics=("arbitrary",), vmem_limit_bytes=VMEM_LIMIT),
        name="merge_ffn",
    )(x2d, y_rnn, y_dn, p2d, p2d, w_rnn, w_dn, w_out, ffn_gain, w_gu, w_down)


def _final_norm_kernel(x_ref, g_ref, o_ref):
    x = x_ref[...]
    ms = jnp.mean(x * x, axis=-1, keepdims=True)
    o_ref[...] = x * lax.rsqrt(ms + EPS) * g_ref[...]


def _final_norm(x2d, gain, *, tm):
    t, d = x2d.shape
    return pl.pallas_call(
        _final_norm_kernel,
        grid=(t // tm,),
        in_specs=[pl.BlockSpec((tm, d), lambda i: (i, 0)), pl.BlockSpec((1, d), lambda i: (0, 0))],
        out_specs=pl.BlockSpec((tm, d), lambda i: (i, 0)),
        out_shape=jax.ShapeDtypeStruct((t, d), F32),
        compiler_params=pltpu.CompilerParams(dimension_semantics=("arbitrary",)),
        name="final_norm",
    )(x2d, gain)


def _pack_w_in(w_in_l):
    split_ba = COL_Z + DN_V
    pad = jnp.zeros((D_MODEL, LANES - N_BA), w_in_l.dtype)
    return jnp.concatenate(
        [w_in_l[:, :split_ba], w_in_l[:, split_ba + N_BA:], w_in_l[:, split_ba:split_ba + N_BA], pad],
        axis=1).astype(BF16)


def _pack_rg_gates(wa, wx, ba, bx):
    w = jnp.concatenate([wa[0], wx[0], wa[1], wx[1]], axis=-1).astype(BF16)
    blocks = lambda v: v.reshape(RNN_BLOCKS, 1, RNN_BLOCK)
    bias = jnp.concatenate([blocks(ba[0]), blocks(bx[0]), blocks(ba[1]), blocks(bx[1])], axis=-1)
    return w, bias.astype(F32)


def _gate_rows(a_log, dt_bias):
    pad = jnp.zeros((LANES - GP_GROUP,), F32)
    row = lambda v: jnp.concatenate([v.reshape(-1).astype(F32), pad]).reshape(1, LANES)
    return row(a_log), row(dt_bias)


def kernel(x, mix_norm, w_in, rg_conv_w, rg_conv_b, rg_wa, rg_ba, rg_wx, rg_bx, rg_lambda, w_rnn_proj,
           dn_conv_w, dn_a_log, dn_dt_bias, dn_norm, w_dn_proj, w_out, ffn_norm, w_gate_up, w_down,
           final_norm):
    b, s, d = x.shape
    depth = w_in.shape[0]
    t = b * s
    assert d == D_MODEL and b % RG_GROUP == 0 and s % (4 * RG_TB) == 0 and s % (DN_G * CHUNK) == 0
    tm = 512 if t % 512 == 0 else 256
    x2d = x.reshape(t, d)
    for l in range(depth):
        p2d = _in_proj(x2d, mix_norm[l].reshape(1, d), _pack_w_in(w_in[l]), tm=tm)
        p3 = p2d.reshape(b, s, N_P)
        alog_row, dtb_row = _gate_rows(dn_a_log[l], dn_dt_bias[l])
        gp, gt = _gate_prep(p3, alog_row, dtb_row)
        gtr = gt.reshape(b, len(GT_ROWS), 2, DN_HEADS, s // CHUNK, CHUNK)
        gt4 = jnp.concatenate([gtr[:, :, 0], gtr[:, :, 1, :, ::-1]], axis=-1)
        rg_w, rg_bias = _pack_rg_gates(rg_wa[l], rg_wx[l], rg_ba[l], rg_bx[l])
        y_rnn = _rglru(p3, rg_conv_w[l], rg_conv_b[l].reshape(1, D_RNN), rg_w, rg_bias, rg_lambda[l])
        y_dn = _deltanet(p3, gp, gt4, dn_conv_w[l], dn_norm[l].reshape(1, DN_DV))
        x2d = _merge_ffn(
            x2d, y_rnn.reshape(t, D_RNN), y_dn.reshape(t, DN_V), p2d,
            w_rnn_proj[l].astype(BF16), w_dn_proj[l].astype(BF16), w_out[l].astype(BF16),
            ffn_norm[l].reshape(1, d), w_gate_up[l].astype(BF16), w_down[l].astype(BF16), tm=tm)
    out = _final_norm(x2d, final_norm.reshape(1, d), tm=tm)
    return out.reshape(b, s, d)
```

```python
import functools

import jax
import jax.numpy as jnp
from jax import lax
from jax.experimental import pallas as pl
from jax.experimental.pallas import tpu as pltpu

F32 = jnp.float32
BF16 = jnp.bfloat16

D_MODEL = 1024
D_RNN = 1536
RNN_BLOCK = 128
RNN_BLOCKS = D_RNN // RNN_BLOCK
RG_C = 8.0
CONV_K = 4
DN_HEADS = 8
DN_DK = 128
DN_DV = 128
DN_QK = DN_HEADS * DN_DK
DN_V = DN_HEADS * DN_DV
DN_QKV = 2 * DN_QK + DN_V
CHUNK = 64
D_FF = 2816
EPS = 1e-6
N_BA = 4 * DN_HEADS
LOG2_E = 1.4426950408889634
TINY = 1e-30

LANES = 128
SUBLANES = 8

COL_RX = 0
COL_RY = D_RNN
COL_QKV = 2 * D_RNN
COL_Z = COL_QKV + DN_QKV
COL_GATES = COL_Z + DN_V
COL_BA = COL_GATES + 2 * D_MODEL
N_P = COL_BA + LANES

VMEM_LIMIT = 56 * 1024 * 1024


def _sigmoid(x):
    return 1.0 / (1.0 + jnp.exp(-x))


def _softplus(x):
    return jnp.maximum(x, 0.0) + jnp.log(1.0 + jnp.exp(-jnp.abs(x)))


def _dot(a, b):
    return jnp.dot(a, b, preferred_element_type=F32)


def _dot_nt(a, b):
    return lax.dot_general(a, b, (((1,), (1,)), ((), ())), preferred_element_type=F32)


def _dot_tn(a, b):
    return lax.dot_general(a, b, (((0,), (0,)), ((), ())), preferred_element_type=F32)


def _aligned(start, multiple):
    return start if isinstance(start, int) else pl.multiple_of(start, multiple)


def _lane_sum(x, ones_bf16):
    return _dot(x.astype(BF16), ones_bf16)


CONV_HALO = SUBLANES


def _zero_halo(pad_ref, s):
    zeros = jnp.zeros((CONV_HALO, pad_ref.shape[1]), F32)
    pad_ref[0:CONV_HALO, :] = zeros
    pad_ref[CONV_HALO + s:2 * CONV_HALO + s, :] = zeros


def _dwconv(pad_ref, x, cw):
    s = x.shape[0]
    pad_ref[CONV_HALO:CONV_HALO + s, :] = x
    y = cw[0:1] * pad_ref[CONV_HALO - 2:CONV_HALO - 2 + s, :]
    for j in range(1, CONV_K):
        y = y + cw[j:j + 1] * pad_ref[CONV_HALO - 2 + j:CONV_HALO - 2 + j + s, :]
    return y


def _in_proj_kernel(x_ref, g_ref, w_ref, o_ref, *, col_chunk):
    x = x_ref[...]
    ms = jnp.mean(x * x, axis=-1, keepdims=True)
    h = (x * lax.rsqrt(ms + EPS) * g_ref[...]).astype(BF16)
    n = o_ref.shape[1]
    for c0 in range(0, n, col_chunk):
        cn = min(col_chunk, n - c0)
        o_ref[:, c0:c0 + cn] = _dot(h, w_ref[:, c0:c0 + cn]).astype(o_ref.dtype)


def _in_proj(x2d, gain, w, *, tm):
    t, d = x2d.shape
    n = w.shape[1]
    return pl.pallas_call(
        functools.partial(_in_proj_kernel, col_chunk=1024),
        grid=(t // tm,),
        in_specs=[
            pl.BlockSpec((tm, d), lambda i: (i, 0)),
            pl.BlockSpec((1, d), lambda i: (0, 0)),
            pl.BlockSpec((d, n), lambda i: (0, 0)),
        ],
        out_specs=pl.BlockSpec((tm, n), lambda i: (i, 0)),
        out_shape=jax.ShapeDtypeStruct((t, n), BF16),
        compiler_params=pltpu.CompilerParams(
            dimension_semantics=("arbitrary",), vmem_limit_bytes=VMEM_LIMIT),
        name="in_proj",
    )(x2d, gain, w)


GP_BETA, GP_GC, GP_EGC, GP_BEGC, GP_EGR, GP_EGT = range(6)
GP_GROUP = 2 * DN_HEADS
GT_ROWS = (GP_BETA, GP_GC, GP_BEGC)
GT_BETA, GT_GC, GT_BEGC = range(3)
DN_HB = 2


def _gate_prep_kernel(ba_ref, alog_ref, dtb_ref, gp_ref, gt_ref):
    s = ba_ref.shape[1]
    x = ba_ref[0].astype(F32)
    lane = lax.broadcasted_iota(jnp.int32, (s, LANES), 1)
    row = lax.broadcasted_iota(jnp.int32, (s, LANES), 0)
    ic = row & (CHUNK - 1)
    beta = _sigmoid(x)
    xs = pltpu.roll(x, LANES - GP_GROUP, axis=1)
    g = -jnp.exp(alog_ref[...]) * _softplus(xs + dtb_ref[...])
    fsum = g
    rsum = g
    sh = 1
    while sh < CHUNK:
        fsum = fsum + jnp.where(ic >= sh, pltpu.roll(fsum, sh, axis=0), 0.0)
        rsum = rsum + jnp.where(ic < CHUNK - sh, pltpu.roll(rsum, s - sh, axis=0), 0.0)
        sh *= 2
    is_fwd = (lane & (GP_GROUP - 1)) < DN_HEADS
    gc = jnp.where(is_fwd, fsum, rsum)
    gr = jnp.where(is_fwd, rsum, fsum) - g
    gtot = fsum + rsum - g
    egc = jnp.exp(gc)
    parts = (beta, gc, egc, beta * egc, jnp.exp(gr), jnp.exp(gtot))
    out = parts[-1]
    out = pltpu.roll(out, (len(parts) - 1) * GP_GROUP, axis=1)
    for qi in range(len(parts) - 2, -1, -1):
        placed = parts[qi] if qi == 0 else pltpu.roll(parts[qi], qi * GP_GROUP, axis=1)
        out = jnp.where(lane < (qi + 1) * GP_GROUP, placed, out)
    for p in range(DN_HEADS // DN_HB):
        gp_ref[0, p] = out if p == 0 else pltpu.roll(out, LANES - p * DN_HB, axis=1)
    out_t = out.T
    for n, quantity in enumerate(GT_ROWS):
        gt_ref[0, n] = out_t[quantity * GP_GROUP:(quantity + 1) * GP_GROUP, :]


def _gate_prep(p3, alog_row, dtb_row):
    b, s, _ = p3.shape
    return pl.pallas_call(
        _gate_prep_kernel,
        grid=(b,),
        in_specs=[
            pl.BlockSpec((1, s, LANES), lambda i: (i, 0, COL_BA // LANES)),
            pl.BlockSpec((1, LANES), lambda i: (0, 0)),
            pl.BlockSpec((1, LANES), lambda i: (0, 0)),
        ],
        out_specs=[
            pl.BlockSpec((1, DN_HEADS // DN_HB, s, LANES), lambda i: (i, 0, 0, 0)),
            pl.BlockSpec((1, len(GT_ROWS), GP_GROUP, s), lambda i: (i, 0, 0, 0)),
        ],
        out_shape=[
            jax.ShapeDtypeStruct((b, DN_HEADS // DN_HB, s, LANES), F32),
            jax.ShapeDtypeStruct((b, len(GT_ROWS), GP_GROUP, s), F32),
        ],
        compiler_params=pltpu.CompilerParams(
            dimension_semantics=("arbitrary",), vmem_limit_bytes=VMEM_LIMIT),
        name="gate_prep",
    )(p3, alog_row, dtb_row)


RG_GROUP = SUBLANES
RG_SLAB_PAD = 8
RG_TB = 32


def _rglru_kernel(rx_ref, ry_ref, cw_ref, cb_ref, w_ref, bias_ref, lam_ref, o_ref, u_s, h_s, pad_s,
                  ab0_s, ab1_s):
    s = rx_ref.shape[1]
    pitch = s + RG_SLAB_PAD
    nb = s // RG_TB
    cw_half = 0.5 * cw_ref[...]
    cb_half = 0.5 * cb_ref[...]
    _zero_halo(pad_s, s)

    def stage(b, carry):
        x = rx_ref[b].astype(F32)
        u_s[pl.ds(pl.multiple_of(b * pitch, SUBLANES), s), :] = _dwconv(pad_s, x, cw_half) + cb_half
        return carry

    lax.fori_loop(0, RG_GROUP, stage, 0)

    c2 = (-0.5 * RG_C * LOG2_E) * _softplus(-lam_ref[...])
    bias_half = 0.5 * bias_ref[...]

    def gather(ref, t0):
        return [ref[pl.ds(t0 + j, RG_GROUP, stride=pitch), :] for j in range(RG_TB)]

    def block_start(i, d):
        return (i if d == 0 else nb - 1 - i) * RG_TB

    def coefficients(i, ab_ref):
        i = jnp.minimum(i, nb - 1)
        for d in range(2):
            hu = jnp.concatenate(gather(u_s, block_start(i, d)), axis=0)
            gts = _dot(hu.astype(BF16), w_ref[:, d * 2 * LANES:(d + 1) * 2 * LANES])
            gts = gts + bias_half[:, d * 2 * LANES:(d + 1) * 2 * LANES]
            t_r = jnp.tanh(gts[:, :LANES])
            t_i = jnp.tanh(gts[:, LANES:])
            a = jnp.exp2(c2[d:d + 1] * t_r + c2[d:d + 1])
            om = 1.0 - a * a
            root = om * lax.rsqrt(jnp.maximum(om, TINY))
            ab_ref[d, 0] = a
            ab_ref[d, 1] = root * ((t_i + 1.0) * hu)

    def scan(i, hs, ab_ref, accumulate):
        out = []
        for d in range(2):
            t0 = block_start(i, d)
            h = hs[d]
            order = range(RG_TB) if d == 0 else range(RG_TB - 1, -1, -1)
            for j in order:
                rows = pl.ds(j * RG_GROUP, RG_GROUP)
                h = ab_ref[d, 0, rows, :] * h + ab_ref[d, 1, rows, :]
                dst = pl.ds(t0 + j, RG_GROUP, stride=pitch)
                if accumulate:
                    h_s[dst, :] = h_s[dst, :] + h
                else:
                    h_s[dst, :] = h
            out.append(h)
        return tuple(out)

    def make_body(accumulate):
        def body(k, hs):
            coefficients(2 * k + 1, ab1_s)
            hs = scan(2 * k, hs, ab0_s, accumulate)
            coefficients(2 * k + 2, ab0_s)
            return scan(2 * k + 1, hs, ab1_s, accumulate)
        return body

    zero = jnp.zeros((RG_GROUP, LANES), F32)
    coefficients(0, ab0_s)
    hs = lax.fori_loop(0, nb // 4, make_body(False), (zero, zero))
    lax.fori_loop(nb // 4, nb // 2, make_body(True), hs)

    def finish(b, carry):
        hsum = h_s[pl.ds(pl.multiple_of(b * pitch, SUBLANES), s), :]
        y = hsum * jax.nn.gelu(ry_ref[b].astype(F32))
        o_ref[b] = y.astype(o_ref.dtype)
        return carry

    lax.fori_loop(0, RG_GROUP, finish, 0)


def _rglru(p3, cw, cb, w, bias, lam):
    b, s, _ = p3.shape
    pitch = s + RG_SLAB_PAD
    blk = (RG_GROUP, s, LANES)
    return pl.pallas_call(
        _rglru_kernel,
        grid=(b // RG_GROUP, RNN_BLOCKS),
        in_specs=[
            pl.BlockSpec(blk, lambda g, n: (g, 0, COL_RX // LANES + n)),
            pl.BlockSpec(blk, lambda g, n: (g, 0, COL_RY // LANES + n)),
            pl.BlockSpec((CONV_K, LANES), lambda g, n: (0, n)),
            pl.BlockSpec((1, LANES), lambda g, n: (0, n)),
            pl.BlockSpec((None, RNN_BLOCK, 4 * RNN_BLOCK), lambda g, n: (n, 0, 0)),
            pl.BlockSpec((None, 1, 4 * RNN_BLOCK), lambda g, n: (n, 0, 0)),
            pl.BlockSpec((2, LANES), lambda g, n: (0, n)),
        ],
        out_specs=pl.BlockSpec(blk, lambda g, n: (g, 0, n)),
        out_shape=jax.ShapeDtypeStruct((b, s, D_RNN), BF16),
        scratch_shapes=[
            pltpu.VMEM((RG_GROUP * pitch, LANES), F32),
            pltpu.VMEM((RG_GROUP * pitch, LANES), F32),
            pltpu.VMEM((s + 2 * CONV_HALO, LANES), F32),
            pltpu.VMEM((2, 2, RG_TB * RG_GROUP, LANES), F32),
            pltpu.VMEM((2, 2, RG_TB * RG_GROUP, LANES), F32),
        ],
        compiler_params=pltpu.CompilerParams(
            dimension_semantics=("arbitrary", "arbitrary"), vmem_limit_bytes=VMEM_LIMIT),
        name="rglru",
    )(p3, p3, cw, cb, w, bias, lam)


DN_G = 4
DN_TR = 256


def _deltanet_kernel(q_ref, k_ref, v_ref, z_ref, gp_ref, gt_ref, cwq_ref, cwk_ref, cwv_ref, ng_ref,
                     o_ref, qn_s, kn_s, v_s, qg_s, kd_s, w_s, u_s, att_s, o_s, pad_s):
    s = q_ref.shape[1]
    nc = s // CHUNK
    head0 = pl.program_id(1) * DN_HB
    ones = jnp.ones((LANES, LANES), BF16)
    for idx in range(3):
        _zero_halo(pad_s.at[idx], s)

    def gate_lane(quantity, d, hh):
        return quantity * GP_GROUP + d * DN_HEADS + hh

    def stage_head(hh):
        lanes = slice(hh * LANES, (hh + 1) * LANES)
        for idx, ref in enumerate((q_ref, k_ref, v_ref)):
            pad_s[idx, CONV_HALO:CONV_HALO + s, :] = ref[0, :, lanes].astype(F32)
        cw_half = [0.5 * cw_ref[:, lanes] for cw_ref in (cwq_ref, cwk_ref, cwv_ref)]

        def seq_tile(i, carry=0):
            r0 = _aligned(i * DN_TR, DN_TR)
            rows = pl.ds(r0, DN_TR)

            def conv_silu(idx):
                cw = cw_half[idx]
                hy = cw[0:1] * pad_s[idx, pl.ds(r0 + CONV_HALO - 2, DN_TR), :]
                for j in range(1, CONV_K):
                    hy = hy + cw[j:j + 1] * pad_s[idx, pl.ds(r0 + CONV_HALO - 2 + j, DN_TR), :]
                return hy * jnp.tanh(hy) + hy

            q = conv_silu(0)
            k = conv_silu(1)
            v = conv_silu(2)
            qn = q * lax.rsqrt(_lane_sum(q * q, ones) + EPS) * (DN_DK ** -0.5)
            kn = k * lax.rsqrt(_lane_sum(k * k, ones) + EPS)
            qn_s[hh, rows, :] = qn.astype(BF16)
            kn_s[hh, rows, :] = kn.astype(BF16)
            v_s[hh, rows, :] = v.astype(BF16)
            gp = gp_ref[0, rows, :]

            def col(quantity, d):
                c = gate_lane(quantity, d, hh)
                return jnp.broadcast_to(gp[:, c:c + 1], (DN_TR, LANES))

            for d in range(2):
                sid = hh * 2 + d
                qg_s[sid, rows, :] = (qn * col(GP_EGC, d)).astype(BF16)
                kd_s[sid, rows, :] = (kn * col(GP_EGR, d)).astype(BF16)
            return carry

        return seq_tile

    ntiles = s // DN_TR
    for hh in range(DN_HB - 1):
        lax.fori_loop(0, ntiles, stage_head(hh), 0)

    row = lax.broadcasted_iota(jnp.int32, (CHUNK, 2 * CHUNK), 0)
    lane = lax.broadcasted_iota(jnp.int32, (CHUNK, 2 * CHUNK), 1)
    is_f = lane < CHUNK
    ahead = jnp.where(is_f, row - lane, lane - CHUNK - row)
    incl = ahead >= 0
    strict = ahead > 0
    eye = (ahead == 0).astype(F32)
    keep_f = is_f.astype(F32).astype(BF16)
    keep_b = (1.0 - is_f.astype(F32)).astype(BF16)
    ztile = jnp.zeros((CHUNK, LANES), BF16)

    def block_diag(xb):
        return jnp.concatenate([xb * keep_f, xb * keep_b], axis=0)

    def diag2(f, b):
        return jnp.concatenate([jnp.concatenate([f, ztile], axis=1), jnp.concatenate([ztile, b], axis=1)], axis=0)

    def gate_row(quantity, hh, j):
        return gt_ref[0, quantity, pl.ds(head0 + hh, 1), pl.ds(j, 1), :][0]

    def prep_stages(it):
        pairs = []
        for hh in range(DN_HB):
            for g in range(DN_G):
                j = it * DN_G + g
                pairs.append((hh, j, _aligned(j * CHUNK, CHUNK), _aligned((nc - 1 - j) * CHUNK, CHUNK)))
        xb, tinv = [], []
        for hh, j, rf, rb in pairs:
            knf = kn_s[hh, pl.ds(rf, CHUNK), :]
            knb = kn_s[hh, pl.ds(rb, CHUNK), :]
            kcat = jnp.concatenate([knf, knb], axis=1)
            qcat = jnp.concatenate([qn_s[hh, pl.ds(rf, CHUNK), :], qn_s[hh, pl.ds(rb, CHUNK), :]], axis=1)
            kdiag = diag2(knf, knb)
            kk = _dot_nt(kcat, kdiag)
            qk = _dot_nt(qcat, kdiag)

            def gcol(quantity):
                lf = gate_lane(quantity, 0, hh)
                lb = gate_lane(quantity, 1, hh)
                f = jnp.broadcast_to(gp_ref[0, pl.ds(rf, CHUNK), lf:lf + 1], (CHUNK, 2 * CHUNK))
                b = jnp.broadcast_to(gp_ref[0, pl.ds(rb, CHUNK), lb:lb + 1], (CHUNK, 2 * CHUNK))
                return jnp.where(is_f, f, b)

            grow = gate_row(GT_GC, hh, j)
            decay = jnp.where(incl, jnp.exp(gcol(GP_GC) - grow), 0.0)
            att_s[hh, pl.ds(rf, CHUNK), :] = (qk * decay).astype(BF16)
            x = jnp.where(strict, -(gcol(GP_BETA) * kk * decay), 0.0)
            xb.append(x.astype(BF16))
            tinv.append(eye + x)
        diag = [block_diag(x) for x in xb]
        yield
        xb = [_dot(x, m).astype(BF16) for x, m in zip(xb, diag)]
        yield
        for k in range(1, 6):
            diag = [block_diag(x) for x in xb]
            if k < 5:
                prod = [_dot(jnp.concatenate([x, t.astype(BF16)], axis=0), m)
                        for x, t, m in zip(xb, tinv, diag)]
                xb = [p[:CHUNK].astype(BF16) for p in prod]
                tinv = [t + p[CHUNK:] for t, p in zip(tinv, prod)]
            else:
                tinv = [t + _dot(t.astype(BF16), m) for t, m in zip(tinv, diag)]
            yield
        for n, (hh, j, rf, rb) in enumerate(pairs):
            sf, sb = hh * 2, hh * 2 + 1
            t_u = (tinv[n] * gate_row(GT_BETA, hh, j)).astype(BF16)
            t_w = (tinv[n] * gate_row(GT_BEGC, hh, j)).astype(BF16)
            u = _dot(t_u, diag2(v_s[hh, pl.ds(rf, CHUNK), :], v_s[hh, pl.ds(rb, CHUNK), :]))
            w = _dot(t_w, diag2(kn_s[hh, pl.ds(rf, CHUNK), :], kn_s[hh, pl.ds(rb, CHUNK), :]))
            u_s[sf, pl.ds(rf, CHUNK), :] = u[:, :DN_DV]
            u_s[sb, pl.ds(rb, CHUNK), :] = u[:, DN_DV:]
            w_s[sf, pl.ds(rf, CHUNK), :] = w[:, :DN_DK].astype(BF16)
            w_s[sb, pl.ds(rb, CHUNK), :] = w[:, DN_DK:].astype(BF16)
        yield

    streams = [(hh, d) for hh in range(DN_HB) for d in range(2)]

    def chunk_rows(i):
        return [_aligned((i if d == 0 else nc - 1 - i) * CHUNK, CHUNK) for _, d in streams]

    def chain_read(i, states):
        res = []
        for (hh, d), r0, st in zip(streams, chunk_rows(i), states):
            sid = hh * 2 + d
            wq = jnp.concatenate([w_s[sid, pl.ds(r0, CHUNK), :], qg_s[sid, pl.ds(r0, CHUNK), :]], axis=0)
            res.append(_dot(wq, st.astype(BF16)))
        return res

    def chain_update(i, states, res):
        new_states = []
        for (hh, d), r0, st, rs in zip(streams, chunk_rows(i), states, res):
            sid = hh * 2 + d
            v_new = (u_s[sid, pl.ds(r0, CHUNK), :] - rs[:CHUNK]).astype(BF16)
            att = att_s[hh, pl.ds(_aligned(i * CHUNK, CHUNK), CHUNK), :]
            v_pad = jnp.concatenate([v_new, ztile] if d == 0 else [ztile, v_new], axis=0)
            o_s[sid, pl.ds(r0, CHUNK), :] = rs[CHUNK:] + _dot(att, v_pad)
            egt_lane = gate_lane(GP_EGT, d, hh)
            egt = gp_ref[0, pl.ds(r0, 1), egt_lane:egt_lane + 1]
            new_states.append(st * egt + _dot_tn(kd_s[sid, pl.ds(r0, CHUNK), :], v_new))
        return tuple(new_states)

    def chain_group(it, states, stages):
        for g in range(DN_G):
            res = chain_read(it * DN_G + g, states)
            next(stages, None)
            states = chain_update(it * DN_G + g, states, res)
            next(stages, None)
        for _ in stages:
            pass
        return states

    def out_tile(i, carry=0):
        rows = pl.ds(_aligned(i * DN_TR, DN_TR), DN_TR)
        for hh in range(DN_HB):
            lanes = slice(hh * LANES, (hh + 1) * LANES)
            o = o_s[hh * 2, rows, :] + o_s[hh * 2 + 1, rows, :]
            var = _lane_sum(o * o, ones) * (1.0 / DN_DV)
            hz = 0.5 * z_ref[0, rows, lanes].astype(F32)
            y = o * lax.rsqrt(var + EPS) * ng_ref[...] * (hz * jnp.tanh(hz) + hz)
            o_ref[0, rows, lanes] = y.astype(o_ref.dtype)
        return carry

    ngroups = nc // DN_G
    edge_tiles = sorted({0, ntiles - 1})
    inner_tiles = range(1, ntiles - 1)

    seq_tile = stage_head(DN_HB - 1)
    for i in edge_tiles:
        seq_tile(i)
    stages = prep_stages(0)
    for i in inner_tiles:
        seq_tile(i)
        next(stages, None)
    for _ in stages:
        pass

    def pipelined(it, states):
        return chain_group(it, states, prep_stages(it + 1))

    zero = jnp.zeros((DN_DK, DN_DV), F32)
    states = lax.fori_loop(0, ngroups - 1, pipelined, tuple(zero for _ in streams))

    def inner_out_tiles():
        for i in inner_tiles:
            out_tile(i)
            yield

    chain_group(ngroups - 1, states, inner_out_tiles())
    for i in edge_tiles:
        out_tile(i)


def _deltanet(p3, gp, gt4, cw, ng):
    b, s, _ = p3.shape
    width = DN_HB * LANES
    blk = (1, s, width)
    hblocks = DN_HEADS // DN_HB
    qkv0 = COL_QKV // width
    ns = 2 * DN_HB
    seq_bf16 = lambda: pltpu.VMEM((ns, s, LANES), BF16)
    seq_f32 = lambda: pltpu.VMEM((ns, s, LANES), F32)
    return pl.pallas_call(
        _deltanet_kernel,
        grid=(b, hblocks),
        in_specs=[
            pl.BlockSpec(blk, lambda i, h: (i, 0, qkv0 + h)),
            pl.BlockSpec(blk, lambda i, h: (i, 0, qkv0 + hblocks + h)),
            pl.BlockSpec(blk, lambda i, h: (i, 0, qkv0 + 2 * hblocks + h)),
            pl.BlockSpec(blk, lambda i, h: (i, 0, COL_Z // width + h)),
            pl.BlockSpec((1, None, s, LANES), lambda i, h: (i, h, 0, 0)),
            pl.BlockSpec((1, len(GT_ROWS), DN_HEADS, s // CHUNK, 2 * CHUNK), lambda i, h: (i, 0, 0, 0, 0)),
            pl.BlockSpec((CONV_K, width), lambda i, h: (0, h)),
            pl.BlockSpec((CONV_K, width), lambda i, h: (0, hblocks + h)),
            pl.BlockSpec((CONV_K, width), lambda i, h: (0, 2 * hblocks + h)),
            pl.BlockSpec((1, LANES), lambda i, h: (0, 0)),
        ],
        out_specs=pl.BlockSpec(blk, lambda i, h: (i, 0, h)),
        out_shape=jax.ShapeDtypeStruct((b, s, DN_V), BF16),
        scratch_shapes=[
            pltpu.VMEM((DN_HB, s, LANES), BF16),
            pltpu.VMEM((DN_HB, s, LANES), BF16),
            pltpu.VMEM((DN_HB, s, LANES), BF16),
            seq_bf16(),
            seq_bf16(),
            seq_bf16(),
            seq_f32(),
            pltpu.VMEM((DN_HB, s, 2 * CHUNK), BF16),
            seq_f32(),
            pltpu.VMEM((3, s + 2 * CONV_HALO, LANES), F32),
        ],
        compiler_params=pltpu.CompilerParams(
            dimension_semantics=("arbitrary", "arbitrary"), vmem_limit_bytes=VMEM_LIMIT),
        name="deltanet",
    )(p3, p3, p3, p3, gp, gt4, cw, cw, cw, ng)


FF_CHUNK = 256


def _merge_ffn_kernel(x_ref, yr_ref, yd_ref, gr_ref, gd_ref, wr_ref, wd_ref, wo_ref, fn_ref, wgu_ref,
                      wdn_ref, last_ref, o_ref, *, apply_last_norm):
    y_rnn = _dot(yr_ref[...], wr_ref[...])
    y_dn = _dot(yd_ref[...], wd_ref[...])
    merged = (_sigmoid(gr_ref[...].astype(F32)) * y_rnn + _sigmoid(gd_ref[...].astype(F32)) * y_dn)
    x1 = x_ref[...] + _dot(merged.astype(BF16), wo_ref[...])
    ms = jnp.mean(x1 * x1, axis=-1, keepdims=True)
    h = (x1 * lax.rsqrt(ms + EPS) * fn_ref[...]).astype(BF16)
    acc = x1
    for c0 in range(0, D_FF, FF_CHUNK):
        gt = _dot(h, wgu_ref[:, c0:c0 + FF_CHUNK])
        up = _dot(h, wgu_ref[:, D_FF + c0:D_FF + c0 + FF_CHUNK])
        act = (gt * _sigmoid(gt) * up).astype(BF16)
        acc = acc + _dot(act, wdn_ref[c0:c0 + FF_CHUNK, :])
    if apply_last_norm:
        ms = jnp.mean(acc * acc, axis=-1, keepdims=True)
        acc = acc * lax.rsqrt(ms + EPS) * last_ref[...]
    o_ref[...] = acc


def _merge_ffn(x2d, y_rnn, y_dn, p2d, w_rnn, w_dn, w_out, ffn_gain, w_gu, w_down, last_gain, *, tm,
               apply_last_norm):
    t, d = x2d.shape
    full = lambda a: pl.BlockSpec(a.shape, lambda i: (0,) * a.ndim)
    return pl.pallas_call(
        functools.partial(_merge_ffn_kernel, apply_last_norm=apply_last_norm),
        grid=(t // tm,),
        in_specs=[
            pl.BlockSpec((tm, d), lambda i: (i, 0)),
            pl.BlockSpec((tm, D_RNN), lambda i: (i, 0)),
            pl.BlockSpec((tm, DN_V), lambda i: (i, 0)),
            pl.BlockSpec((tm, d), lambda i: (i, COL_GATES // D_MODEL)),
            pl.BlockSpec((tm, d), lambda i: (i, COL_GATES // D_MODEL + 1)),
            full(w_rnn), full(w_dn), full(w_out), full(ffn_gain), full(w_gu), full(w_down), full(last_gain),
        ],
        out_specs=pl.BlockSpec((tm, d), lambda i: (i, 0)),
        out_shape=jax.ShapeDtypeStruct((t, d), F32),
        compiler_params=pltpu.CompilerParams(
            dimension_semantics=("arbitrary",), vmem_limit_bytes=VMEM_LIMIT),
        name="merge_ffn",
    )(x2d, y_rnn, y_dn, p2d, p2d, w_rnn, w_dn, w_out, ffn_gain, w_gu, w_down, last_gain)


def _pack_w_in(w_in_l):
    split_ba = COL_Z + DN_V
    w = w_in_l.astype(BF16)
    pad = jnp.zeros((D_MODEL, LANES - N_BA), BF16)
    return jnp.concatenate(
        [w[:, :split_ba], w[:, split_ba + N_BA:], w[:, split_ba:split_ba + N_BA], pad], axis=1)


def _pack_rg_gates(wa, wx, ba, bx):
    w = jnp.concatenate([wa[0], wx[0], wa[1], wx[1]], axis=-1).astype(BF16)
    blocks = lambda v: v.reshape(RNN_BLOCKS, 1, RNN_BLOCK)
    bias = jnp.concatenate([blocks(ba[0]), blocks(bx[0]), blocks(ba[1]), blocks(bx[1])], axis=-1)
    return w, bias.astype(F32)


def _gate_rows(a_log, dt_bias):
    pad = jnp.zeros((LANES - GP_GROUP,), F32)
    row = lambda v: jnp.concatenate([v.reshape(-1).astype(F32), pad]).reshape(1, LANES)
    return row(a_log), row(dt_bias)


def kernel(x, mix_norm, w_in, rg_conv_w, rg_conv_b, rg_wa, rg_ba, rg_wx, rg_bx, rg_lambda, w_rnn_proj,
           dn_conv_w, dn_a_log, dn_dt_bias, dn_norm, w_dn_proj, w_out, ffn_norm, w_gate_up, w_down,
           final_norm):
    b, s, d = x.shape
    depth = w_in.shape[0]
    t = b * s
    assert d == D_MODEL and b % RG_GROUP == 0 and s % (4 * RG_TB) == 0 and s % DN_TR == 0
    assert DN_TR == DN_G * CHUNK
    tm = 512 if t % 512 == 0 else 256
    x2d = x.reshape(t, d)
    for l in range(depth):
        p2d = _in_proj(x2d, mix_norm[l].reshape(1, d), _pack_w_in(w_in[l]), tm=tm)
        p3 = p2d.reshape(b, s, N_P)
        alog_row, dtb_row = _gate_rows(dn_a_log[l], dn_dt_bias[l])
        gp, gt = _gate_prep(p3, alog_row, dtb_row)
        gtr = gt.reshape(b, len(GT_ROWS), 2, DN_HEADS, s // CHUNK, CHUNK)
        gt4 = jnp.concatenate([gtr[:, :, 0], gtr[:, :, 1, :, ::-1]], axis=-1)
        rg_w, rg_bias = _pack_rg_gates(rg_wa[l], rg_wx[l], rg_ba[l], rg_bx[l])
        y_rnn = _rglru(p3, rg_conv_w[l], rg_conv_b[l].reshape(1, D_RNN), rg_w, rg_bias, rg_lambda[l])
        y_dn = _deltanet(p3, gp, gt4, dn_conv_w[l], dn_norm[l].reshape(1, DN_DV))
        x2d = _merge_ffn(
            x2d, y_rnn.reshape(t, D_RNN), y_dn.reshape(t, DN_V), p2d,
            w_rnn_proj[l].astype(BF16), w_dn_proj[l].astype(BF16), w_out[l].astype(BF16),
            ffn_norm[l].reshape(1, d), w_gate_up[l].astype(BF16), w_down[l].astype(BF16),
            final_norm.reshape(1, d), tm=tm, apply_last_norm=(l == depth - 1))
    return x2d.reshape(b, s, d)
```

```python
import functools

import jax
import jax.numpy as jnp
from jax import lax
from jax.experimental import pallas as pl
from jax.experimental.pallas import tpu as pltpu

F32 = jnp.float32
BF16 = jnp.bfloat16

D_MODEL = 1024
D_RNN = 1536
RNN_BLOCK = 128
RNN_BLOCKS = D_RNN // RNN_BLOCK
RG_C = 8.0
CONV_K = 4
DN_HEADS = 8
DN_DK = 128
DN_DV = 128
DN_QK = DN_HEADS * DN_DK
DN_V = DN_HEADS * DN_DV
DN_QKV = 2 * DN_QK + DN_V
CHUNK = 64
D_FF = 2816
EPS = 1e-6
N_BA = 4 * DN_HEADS
LOG2_E = 1.4426950408889634
TINY = 1e-30

LANES = 128
SUBLANES = 8

COL_RX = 0
COL_RY = D_RNN
COL_QKV = 2 * D_RNN
COL_Z = COL_QKV + DN_QKV
COL_GATES = COL_Z + DN_V
COL_BA = COL_GATES + 2 * D_MODEL
N_P = COL_BA + LANES

VMEM_LIMIT = 56 * 1024 * 1024


def _sigmoid(x):
    return 1.0 / (1.0 + jnp.exp(-x))


def _softplus(x):
    return jnp.maximum(x, 0.0) + jnp.log(1.0 + jnp.exp(-jnp.abs(x)))


def _dot(a, b):
    return jnp.dot(a, b, preferred_element_type=F32)


def _dot_nt(a, b):
    return lax.dot_general(a, b, (((1,), (1,)), ((), ())), preferred_element_type=F32)


def _dot_tn(a, b):
    return lax.dot_general(a, b, (((0,), (0,)), ((), ())), preferred_element_type=F32)


def _aligned(start, multiple):
    return start if isinstance(start, int) else pl.multiple_of(start, multiple)


def _lane_sum(x, ones_bf16):
    return _dot(x.astype(BF16), ones_bf16)


CONV_HALO = SUBLANES


def _zero_halo(pad_ref, s):
    zeros = jnp.zeros((CONV_HALO, pad_ref.shape[1]), F32)
    pad_ref[0:CONV_HALO, :] = zeros
    pad_ref[CONV_HALO + s:2 * CONV_HALO + s, :] = zeros


def _dwconv(pad_ref, x, cw):
    s = x.shape[0]
    pad_ref[CONV_HALO:CONV_HALO + s, :] = x
    y = cw[0:1] * pad_ref[CONV_HALO - 2:CONV_HALO - 2 + s, :]
    for j in range(1, CONV_K):
        y = y + cw[j:j + 1] * pad_ref[CONV_HALO - 2 + j:CONV_HALO - 2 + j + s, :]
    return y


def _in_proj_kernel(x_ref, g_ref, w_ref, o_ref, *, col_chunk):
    x = x_ref[...]
    ms = jnp.mean(x * x, axis=-1, keepdims=True)
    h = (x * lax.rsqrt(ms + EPS) * g_ref[...]).astype(BF16)
    n = o_ref.shape[1]
    for c0 in range(0, n, col_chunk):
        cn = min(col_chunk, n - c0)
        o_ref[:, c0:c0 + cn] = _dot(h, w_ref[:, c0:c0 + cn]).astype(o_ref.dtype)


def _in_proj(x2d, gain, w, *, tm):
    t, d = x2d.shape
    n = w.shape[1]
    return pl.pallas_call(
        functools.partial(_in_proj_kernel, col_chunk=1024),
        grid=(t // tm,),
        in_specs=[
            pl.BlockSpec((tm, d), lambda i: (i, 0)),
            pl.BlockSpec((1, d), lambda i: (0, 0)),
            pl.BlockSpec((d, n), lambda i: (0, 0)),
        ],
        out_specs=pl.BlockSpec((tm, n), lambda i: (i, 0)),
        out_shape=jax.ShapeDtypeStruct((t, n), BF16),
        compiler_params=pltpu.CompilerParams(
            dimension_semantics=("arbitrary",), vmem_limit_bytes=VMEM_LIMIT),
        name="in_proj",
    )(x2d, gain, w)


GP_BETA, GP_GC, GP_EGC, GP_BEGC, GP_EGR, GP_EGT = range(6)
GP_GROUP = 2 * DN_HEADS
GT_ROWS = (GP_BETA, GP_GC, GP_BEGC)
GT_BETA, GT_GC, GT_BEGC = range(3)
DN_HB = 2


def _gate_prep_kernel(ba_ref, alog_ref, dtb_ref, gp_ref, gt_ref):
    s = ba_ref.shape[1]
    x = ba_ref[0].astype(F32)
    lane = lax.broadcasted_iota(jnp.int32, (s, LANES), 1)
    row = lax.broadcasted_iota(jnp.int32, (s, LANES), 0)
    ic = row & (CHUNK - 1)
    beta = _sigmoid(x)
    xs = pltpu.roll(x, LANES - GP_GROUP, axis=1)
    g = -jnp.exp(alog_ref[...]) * _softplus(xs + dtb_ref[...])
    fsum = g
    rsum = g
    sh = 1
    while sh < CHUNK:
        fsum = fsum + jnp.where(ic >= sh, pltpu.roll(fsum, sh, axis=0), 0.0)
        rsum = rsum + jnp.where(ic < CHUNK - sh, pltpu.roll(rsum, s - sh, axis=0), 0.0)
        sh *= 2
    is_fwd = (lane & (GP_GROUP - 1)) < DN_HEADS
    gc = jnp.where(is_fwd, fsum, rsum)
    gr = jnp.where(is_fwd, rsum, fsum) - g
    gtot = fsum + rsum - g
    egc = jnp.exp(gc)
    parts = (beta, gc, egc, beta * egc, jnp.exp(gr), jnp.exp(gtot))
    out = parts[-1]
    out = pltpu.roll(out, (len(parts) - 1) * GP_GROUP, axis=1)
    for qi in range(len(parts) - 2, -1, -1):
        placed = parts[qi] if qi == 0 else pltpu.roll(parts[qi], qi * GP_GROUP, axis=1)
        out = jnp.where(lane < (qi + 1) * GP_GROUP, placed, out)
    for p in range(DN_HEADS // DN_HB):
        gp_ref[0, p] = out if p == 0 else pltpu.roll(out, LANES - p * DN_HB, axis=1)
    out_t = out.T
    for n, quantity in enumerate(GT_ROWS):
        gt_ref[0, n] = out_t[quantity * GP_GROUP:(quantity + 1) * GP_GROUP, :]


def _gate_prep(p3, alog_row, dtb_row):
    b, s, _ = p3.shape
    return pl.pallas_call(
        _gate_prep_kernel,
        grid=(b,),
        in_specs=[
            pl.BlockSpec((1, s, LANES), lambda i: (i, 0, COL_BA // LANES)),
            pl.BlockSpec((1, LANES), lambda i: (0, 0)),
            pl.BlockSpec((1, LANES), lambda i: (0, 0)),
        ],
        out_specs=[
            pl.BlockSpec((1, DN_HEADS // DN_HB, s, LANES), lambda i: (i, 0, 0, 0)),
            pl.BlockSpec((1, len(GT_ROWS), GP_GROUP, s), lambda i: (i, 0, 0, 0)),
        ],
        out_shape=[
            jax.ShapeDtypeStruct((b, DN_HEADS // DN_HB, s, LANES), F32),
            jax.ShapeDtypeStruct((b, len(GT_ROWS), GP_GROUP, s), F32),
        ],
        compiler_params=pltpu.CompilerParams(
            dimension_semantics=("arbitrary",), vmem_limit_bytes=VMEM_LIMIT),
        name="gate_prep",
    )(p3, alog_row, dtb_row)


RG_GROUP = SUBLANES
RG_SLAB_PAD = 8
RG_TB = 32


def _rglru_kernel(rx_ref, ry_ref, cw_ref, cb_ref, w_ref, bias_ref, lam_ref, o_ref, u_s, h_s, pad_s,
                  ab0_s, ab1_s):
    s = rx_ref.shape[1]
    pitch = s + RG_SLAB_PAD
    nb = s // RG_TB
    cw_half = 0.5 * cw_ref[...]
    cb_half = 0.5 * cb_ref[...]
    _zero_halo(pad_s, s)

    def stage(b, carry):
        x = rx_ref[b].astype(F32)
        u_s[pl.ds(pl.multiple_of(b * pitch, SUBLANES), s), :] = _dwconv(pad_s, x, cw_half) + cb_half
        return carry

    lax.fori_loop(0, RG_GROUP, stage, 0)

    c2 = (-0.5 * RG_C * LOG2_E) * _softplus(-lam_ref[...])
    bias_half = 0.5 * bias_ref[...]

    def gather(ref, t0):
        return [ref[pl.ds(t0 + j, RG_GROUP, stride=pitch), :] for j in range(RG_TB)]

    def block_start(i, d):
        return (i if d == 0 else nb - 1 - i) * RG_TB

    def coefficients(i, ab_ref):
        i = jnp.minimum(i, nb - 1)
        for d in range(2):
            hu = jnp.concatenate(gather(u_s, block_start(i, d)), axis=0)
            gts = _dot(hu.astype(BF16), w_ref[:, d * 2 * LANES:(d + 1) * 2 * LANES])
            gts = gts + bias_half[:, d * 2 * LANES:(d + 1) * 2 * LANES]
            t_r = jnp.tanh(gts[:, :LANES])
            t_i = jnp.tanh(gts[:, LANES:])
            a = jnp.exp2(c2[d:d + 1] * t_r + c2[d:d + 1])
            om = 1.0 - a * a
            root = om * lax.rsqrt(jnp.maximum(om, TINY))
            ab_ref[d, 0] = a
            ab_ref[d, 1] = root * ((t_i + 1.0) * hu)

    def scan(i, hs, ab_ref, accumulate):
        out = []
        for d in range(2):
            t0 = block_start(i, d)
            h = hs[d]
            order = range(RG_TB) if d == 0 else range(RG_TB - 1, -1, -1)
            for j in order:
                rows = pl.ds(j * RG_GROUP, RG_GROUP)
                h = ab_ref[d, 0, rows, :] * h + ab_ref[d, 1, rows, :]
                dst = pl.ds(t0 + j, RG_GROUP, stride=pitch)
                if accumulate:
                    h_s[dst, :] = h_s[dst, :] + h
                else:
                    h_s[dst, :] = h
            out.append(h)
        return tuple(out)

    def make_body(accumulate):
        def body(k, hs):
            coefficients(2 * k + 1, ab1_s)
            hs = scan(2 * k, hs, ab0_s, accumulate)
            coefficients(2 * k + 2, ab0_s)
            return scan(2 * k + 1, hs, ab1_s, accumulate)
        return body

    zero = jnp.zeros((RG_GROUP, LANES), F32)
    coefficients(0, ab0_s)
    hs = lax.fori_loop(0, nb // 4, make_body(False), (zero, zero))
    lax.fori_loop(nb // 4, nb // 2, make_body(True), hs)

    def finish(b, carry):
        hsum = h_s[pl.ds(pl.multiple_of(b * pitch, SUBLANES), s), :]
        y = hsum * jax.nn.gelu(ry_ref[b].astype(F32))
        o_ref[b] = y.astype(o_ref.dtype)
        return carry

    lax.fori_loop(0, RG_GROUP, finish, 0)


def _rglru(p3, cw, cb, w, bias, lam):
    b, s, _ = p3.shape
    pitch = s + RG_SLAB_PAD
    blk = (RG_GROUP, s, LANES)
    return pl.pallas_call(
        _rglru_kernel,
        grid=(b // RG_GROUP, RNN_BLOCKS),
        in_specs=[
            pl.BlockSpec(blk, lambda g, n: (g, 0, COL_RX // LANES + n)),
            pl.BlockSpec(blk, lambda g, n: (g, 0, COL_RY // LANES + n)),
            pl.BlockSpec((CONV_K, LANES), lambda g, n: (0, n)),
            pl.BlockSpec((1, LANES), lambda g, n: (0, n)),
            pl.BlockSpec((None, RNN_BLOCK, 4 * RNN_BLOCK), lambda g, n: (n, 0, 0)),
            pl.BlockSpec((None, 1, 4 * RNN_BLOCK), lambda g, n: (n, 0, 0)),
            pl.BlockSpec((2, LANES), lambda g, n: (0, n)),
        ],
        out_specs=pl.BlockSpec(blk, lambda g, n: (g, 0, n)),
        out_shape=jax.ShapeDtypeStruct((b, s, D_RNN), BF16),
        scratch_shapes=[
            pltpu.VMEM((RG_GROUP * pitch, LANES), F32),
            pltpu.VMEM((RG_GROUP * pitch, LANES), F32),
            pltpu.VMEM((s + 2 * CONV_HALO, LANES), F32),
            pltpu.VMEM((2, 2, RG_TB * RG_GROUP, LANES), F32),
            pltpu.VMEM((2, 2, RG_TB * RG_GROUP, LANES), F32),
        ],
        compiler_params=pltpu.CompilerParams(
            dimension_semantics=("arbitrary", "arbitrary"), vmem_limit_bytes=VMEM_LIMIT),
        name="rglru",
    )(p3, p3, cw, cb, w, bias, lam)


DN_G = 4
DN_TR = 256


def _deltanet_kernel(q_ref, k_ref, v_ref, z_ref, gp_ref, gt_ref, cwq_ref, cwk_ref, cwv_ref, ng_ref,
                     o_ref, qn_s, kn_s, v_s, qg_s, kd_s, w_s, u_s, att_s, o_s, pad_s):
    s = q_ref.shape[1]
    nc = s // CHUNK
    head0 = pl.program_id(1) * DN_HB
    ones = jnp.ones((LANES, LANES), BF16)
    for idx in range(3 * DN_HB):
        _zero_halo(pad_s.at[idx], s)

    def gate_lane(quantity, d, hh):
        return quantity * GP_GROUP + d * DN_HEADS + hh

    def stage_head(hh):
        lanes = slice(hh * LANES, (hh + 1) * LANES)
        for idx, ref in enumerate((q_ref, k_ref, v_ref)):
            pad_s[3 * hh + idx, CONV_HALO:CONV_HALO + s, :] = ref[0, :, lanes].astype(F32)
        cw_half = [0.5 * cw_ref[:, lanes] for cw_ref in (cwq_ref, cwk_ref, cwv_ref)]

        def seq_tile(i, carry=0):
            r0 = _aligned(i * DN_TR, DN_TR)
            rows = pl.ds(r0, DN_TR)

            def conv_silu(idx):
                cw = cw_half[idx]
                pad = pad_s.at[3 * hh + idx]
                hy = cw[0:1] * pad[pl.ds(r0 + CONV_HALO - 2, DN_TR), :]
                for j in range(1, CONV_K):
                    hy = hy + cw[j:j + 1] * pad[pl.ds(r0 + CONV_HALO - 2 + j, DN_TR), :]
                return hy * jnp.tanh(hy) + hy

            q = conv_silu(0)
            k = conv_silu(1)
            v = conv_silu(2)
            qn = q * lax.rsqrt(_lane_sum(q * q, ones) + EPS) * (DN_DK ** -0.5)
            kn = k * lax.rsqrt(_lane_sum(k * k, ones) + EPS)
            qn_s[hh, rows, :] = qn.astype(BF16)
            kn_s[hh, rows, :] = kn.astype(BF16)
            v_s[hh, rows, :] = v.astype(BF16)
            gp = gp_ref[0, rows, :]

            def col(quantity, d):
                c = gate_lane(quantity, d, hh)
                return jnp.broadcast_to(gp[:, c:c + 1], (DN_TR, LANES))

            for d in range(2):
                sid = hh * 2 + d
                qg_s[sid, rows, :] = (qn * col(GP_EGC, d)).astype(BF16)
                kd_s[sid, rows, :] = (kn * col(GP_EGR, d)).astype(BF16)
            return carry

        return seq_tile

    seq_tiles = [stage_head(hh) for hh in range(DN_HB)]

    row = lax.broadcasted_iota(jnp.int32, (CHUNK, 2 * CHUNK), 0)
    lane = lax.broadcasted_iota(jnp.int32, (CHUNK, 2 * CHUNK), 1)
    is_f = lane < CHUNK
    ahead = jnp.where(is_f, row - lane, lane - CHUNK - row)
    incl = ahead >= 0
    strict = ahead > 0
    eye = (ahead == 0).astype(F32)
    keep_f = is_f.astype(F32).astype(BF16)
    keep_b = (1.0 - is_f.astype(F32)).astype(BF16)
    ztile = jnp.zeros((CHUNK, LANES), BF16)

    def block_diag(xb):
        return jnp.concatenate([xb * keep_f, xb * keep_b], axis=0)

    def diag2(f, b):
        return jnp.concatenate([jnp.concatenate([f, ztile], axis=1), jnp.concatenate([ztile, b], axis=1)], axis=0)

    def gate_row(quantity, hh, j):
        return gt_ref[0, quantity, pl.ds(head0 + hh, 1), pl.ds(j, 1), :][0]

    def prep_stages(it):
        pairs = []
        for hh in range(DN_HB):
            for g in range(DN_G):
                j = it * DN_G + g
                pairs.append((hh, j, _aligned(j * CHUNK, CHUNK), _aligned((nc - 1 - j) * CHUNK, CHUNK)))
        xb, tinv = [], []
        for hh, j, rf, rb in pairs:
            knf = kn_s[hh, pl.ds(rf, CHUNK), :]
            knb = kn_s[hh, pl.ds(rb, CHUNK), :]
            kcat = jnp.concatenate([knf, knb], axis=1)
            qcat = jnp.concatenate([qn_s[hh, pl.ds(rf, CHUNK), :], qn_s[hh, pl.ds(rb, CHUNK), :]], axis=1)
            kdiag = diag2(knf, knb)
            kk = _dot_nt(kcat, kdiag)
            qk = _dot_nt(qcat, kdiag)

            def gcol(quantity):
                lf = gate_lane(quantity, 0, hh)
                lb = gate_lane(quantity, 1, hh)
                f = jnp.broadcast_to(gp_ref[0, pl.ds(rf, CHUNK), lf:lf + 1], (CHUNK, 2 * CHUNK))
                b = jnp.broadcast_to(gp_ref[0, pl.ds(rb, CHUNK), lb:lb + 1], (CHUNK, 2 * CHUNK))
                return jnp.where(is_f, f, b)

            grow = gate_row(GT_GC, hh, j)
            decay = jnp.where(incl, jnp.exp(gcol(GP_GC) - grow), 0.0)
            att_s[hh, pl.ds(rf, CHUNK), :] = (qk * decay).astype(BF16)
            x = jnp.where(strict, -(gcol(GP_BETA) * kk * decay), 0.0)
            xb.append(x.astype(BF16))
            tinv.append(eye + x)
        diag = [block_diag(x) for x in xb]
        yield
        xb = [_dot(x, m).astype(BF16) for x, m in zip(xb, diag)]
        yield
        for k in range(1, 6):
            diag = [block_diag(x) for x in xb]
            if k < 5:
                prod = [_dot(jnp.concatenate([x, t.astype(BF16)], axis=0), m)
                        for x, t, m in zip(xb, tinv, diag)]
                xb = [p[:CHUNK].astype(BF16) for p in prod]
                tinv = [t + p[CHUNK:] for t, p in zip(tinv, prod)]
            else:
                tinv = [t + _dot(t.astype(BF16), m) for t, m in zip(tinv, diag)]
            yield
        for n, (hh, j, rf, rb) in enumerate(pairs):
            sf, sb = hh * 2, hh * 2 + 1
            t_u = (tinv[n] * gate_row(GT_BETA, hh, j)).astype(BF16)
            t_w = (tinv[n] * gate_row(GT_BEGC, hh, j)).astype(BF16)
            u = _dot(t_u, diag2(v_s[hh, pl.ds(rf, CHUNK), :], v_s[hh, pl.ds(rb, CHUNK), :]))
            w = _dot(t_w, diag2(kn_s[hh, pl.ds(rf, CHUNK), :], kn_s[hh, pl.ds(rb, CHUNK), :]))
            u_s[sf, pl.ds(rf, CHUNK), :] = u[:, :DN_DV]
            u_s[sb, pl.ds(rb, CHUNK), :] = u[:, DN_DV:]
            w_s[sf, pl.ds(rf, CHUNK), :] = w[:, :DN_DK].astype(BF16)
            w_s[sb, pl.ds(rb, CHUNK), :] = w[:, DN_DK:].astype(BF16)
        yield

    streams = [(hh, d) for hh in range(DN_HB) for d in range(2)]

    def chunk_rows(i):
        return [_aligned((i if d == 0 else nc - 1 - i) * CHUNK, CHUNK) for _, d in streams]

    def chain_read(i, states):
        res = []
        for (hh, d), r0, st in zip(streams, chunk_rows(i), states):
            sid = hh * 2 + d
            wq = jnp.concatenate([w_s[sid, pl.ds(r0, CHUNK), :], qg_s[sid, pl.ds(r0, CHUNK), :]], axis=0)
            res.append(_dot(wq, st.astype(BF16)))
        return res

    def chain_update(i, states, res):
        new_states = []
        for (hh, d), r0, st, rs in zip(streams, chunk_rows(i), states, res):
            sid = hh * 2 + d
            v_new = (u_s[sid, pl.ds(r0, CHUNK), :] - rs[:CHUNK]).astype(BF16)
            att = att_s[hh, pl.ds(_aligned(i * CHUNK, CHUNK), CHUNK), :]
            v_pad = jnp.concatenate([v_new, ztile] if d == 0 else [ztile, v_new], axis=0)
            o_s[sid, pl.ds(r0, CHUNK), :] = rs[CHUNK:] + _dot(att, v_pad)
            egt_lane = gate_lane(GP_EGT, d, hh)
            egt = gp_ref[0, pl.ds(r0, 1), egt_lane:egt_lane + 1]
            new_states.append(st * egt + _dot_tn(kd_s[sid, pl.ds(r0, CHUNK), :], v_new))
        return tuple(new_states)

    def chain_group(it, states, *fillers):
        def advance():
            for f in fillers:
                next(f, None)
        for g in range(DN_G):
            res = chain_read(it * DN_G + g, states)
            advance()
            states = chain_update(it * DN_G + g, states, res)
            advance()
        for f in fillers:
            for _ in f:
                pass
        return states

    def out_tile(i, carry=0):
        rows = pl.ds(_aligned(i * DN_TR, DN_TR), DN_TR)
        for hh in range(DN_HB):
            lanes = slice(hh * LANES, (hh + 1) * LANES)
            o = o_s[hh * 2, rows, :] + o_s[hh * 2 + 1, rows, :]
            var = _lane_sum(o * o, ones) * (1.0 / DN_DV)
            hz = 0.5 * z_ref[0, rows, lanes].astype(F32)
            y = o * lax.rsqrt(var + EPS) * ng_ref[...] * (hz * jnp.tanh(hz) + hz)
            o_ref[0, rows, lanes] = y.astype(o_ref.dtype)
        return carry

    n = nc // DN_G

    def seq_items(g):
        if g <= n - 1 - g:
            for t in sorted({g, n - 1 - g}):
                for tile in seq_tiles:
                    tile(t)
                    yield

    def out_items(g):
        if 0 <= n - 1 - g <= g:
            for t in sorted({g, n - 1 - g}):
                out_tile(t)
                yield

    def run(*generators):
        live = list(generators)
        while live:
            live = [g for g in live if next(g, StopIteration) is not StopIteration]

    run(seq_items(0))
    run(prep_stages(0), seq_items(1))
    states = tuple(jnp.zeros((DN_DK, DN_DV), F32) for _ in streams)
    for g in range(n):
        fillers = [seq_items(g + 2), out_items(g - 1)]
        if g + 1 < n:
            fillers.insert(0, prep_stages(g + 1))
        states = chain_group(g, states, *fillers)
    run(out_items(n - 1))


def _deltanet(p3, gp, gt4, cw, ng):
    b, s, _ = p3.shape
    width = DN_HB * LANES
    blk = (1, s, width)
    hblocks = DN_HEADS // DN_HB
    qkv0 = COL_QKV // width
    ns = 2 * DN_HB
    seq_bf16 = lambda: pltpu.VMEM((ns, s, LANES), BF16)
    seq_f32 = lambda: pltpu.VMEM((ns, s, LANES), F32)
    return pl.pallas_call(
        _deltanet_kernel,
        grid=(b, hblocks),
        in_specs=[
            pl.BlockSpec(blk, lambda i, h: (i, 0, qkv0 + h)),
            pl.BlockSpec(blk, lambda i, h: (i, 0, qkv0 + hblocks + h)),
            pl.BlockSpec(blk, lambda i, h: (i, 0, qkv0 + 2 * hblocks + h)),
            pl.BlockSpec(blk, lambda i, h: (i, 0, COL_Z // width + h)),
            pl.BlockSpec((1, None, s, LANES), lambda i, h: (i, h, 0, 0)),
            pl.BlockSpec((1, len(GT_ROWS), DN_HEADS, s // CHUNK, 2 * CHUNK), lambda i, h: (i, 0, 0, 0, 0)),
            pl.BlockSpec((CONV_K, width), lambda i, h: (0, h)),
            pl.BlockSpec((CONV_K, width), lambda i, h: (0, hblocks + h)),
            pl.BlockSpec((CONV_K, width), lambda i, h: (0, 2 * hblocks + h)),
            pl.BlockSpec((1, LANES), lambda i, h: (0, 0)),
        ],
        out_specs=pl.BlockSpec(blk, lambda i, h: (i, 0, h)),
        out_shape=jax.ShapeDtypeStruct((b, s, DN_V), BF16),
        scratch_shapes=[
            pltpu.VMEM((DN_HB, s, LANES), BF16),
            pltpu.VMEM((DN_HB, s, LANES), BF16),
            pltpu.VMEM((DN_HB, s, LANES), BF16),
            seq_bf16(),
            seq_bf16(),
            seq_bf16(),
            seq_f32(),
            pltpu.VMEM((DN_HB, s, 2 * CHUNK), BF16),
            seq_f32(),
            pltpu.VMEM((3 * DN_HB, s + 2 * CONV_HALO, LANES), F32),
        ],
        compiler_params=pltpu.CompilerParams(
            dimension_semantics=("arbitrary", "arbitrary"), vmem_limit_bytes=VMEM_LIMIT),
        name="deltanet",
    )(p3, p3, p3, p3, gp, gt4, cw, cw, cw, ng)


FF_CHUNK = 256


def _merge_ffn_kernel(x_ref, yr_ref, yd_ref, gr_ref, gd_ref, wr_ref, wd_ref, wo_ref, fn_ref, wgu_ref,
                      wdn_ref, last_ref, o_ref, *, apply_last_norm):
    y_rnn = _dot(yr_ref[...], wr_ref[...])
    y_dn = _dot(yd_ref[...], wd_ref[...])
    merged = (_sigmoid(gr_ref[...].astype(F32)) * y_rnn + _sigmoid(gd_ref[...].astype(F32)) * y_dn)
    x1 = x_ref[...] + _dot(merged.astype(BF16), wo_ref[...])
    ms = jnp.mean(x1 * x1, axis=-1, keepdims=True)
    h = (x1 * lax.rsqrt(ms + EPS) * fn_ref[...]).astype(BF16)
    acc = x1
    for c0 in range(0, D_FF, FF_CHUNK):
        gt = _dot(h, wgu_ref[:, c0:c0 + FF_CHUNK])
        up = _dot(h, wgu_ref[:, D_FF + c0:D_FF + c0 + FF_CHUNK])
        act = (gt * _sigmoid(gt) * up).astype(BF16)
        acc = acc + _dot(act, wdn_ref[c0:c0 + FF_CHUNK, :])
    if apply_last_norm:
        ms = jnp.mean(acc * acc, axis=-1, keepdims=True)
        acc = acc * lax.rsqrt(ms + EPS) * last_ref[...]
    o_ref[...] = acc


def _merge_ffn(x2d, y_rnn, y_dn, p2d, w_rnn, w_dn, w_out, ffn_gain, w_gu, w_down, last_gain, *, tm,
               apply_last_norm):
    t, d = x2d.shape
    full = lambda a: pl.BlockSpec(a.shape, lambda i: (0,) * a.ndim)
    return pl.pallas_call(
        functools.partial(_merge_ffn_kernel, apply_last_norm=apply_last_norm),
        grid=(t // tm,),
        in_specs=[
            pl.BlockSpec((tm, d), lambda i: (i, 0)),
            pl.BlockSpec((tm, D_RNN), lambda i: (i, 0)),
            pl.BlockSpec((tm, DN_V), lambda i: (i, 0)),
            pl.BlockSpec((tm, d), lambda i: (i, COL_GATES // D_MODEL)),
            pl.BlockSpec((tm, d), lambda i: (i, COL_GATES // D_MODEL + 1)),
            full(w_rnn), full(w_dn), full(w_out), full(ffn_gain), full(w_gu), full(w_down), full(last_gain),
        ],
        out_specs=pl.BlockSpec((tm, d), lambda i: (i, 0)),
        out_shape=jax.ShapeDtypeStruct((t, d), F32),
        compiler_params=pltpu.CompilerParams(
            dimension_semantics=("arbitrary",), vmem_limit_bytes=VMEM_LIMIT),
        name="merge_ffn",
    )(x2d, y_rnn, y_dn, p2d, p2d, w_rnn, w_dn, w_out, ffn_gain, w_gu, w_down, last_gain)


def _pack_w_in(w_in_l):
    split_ba = COL_Z + DN_V
    w = w_in_l.astype(BF16)
    pad = jnp.zeros((D_MODEL, LANES - N_BA), BF16)
    return jnp.concatenate(
        [w[:, :split_ba], w[:, split_ba + N_BA:], w[:, split_ba:split_ba + N_BA], pad], axis=1)


def _pack_rg_gates(wa, wx, ba, bx):
    w = jnp.concatenate([wa[0], wx[0], wa[1], wx[1]], axis=-1).astype(BF16)
    blocks = lambda v: v.reshape(RNN_BLOCKS, 1, RNN_BLOCK)
    bias = jnp.concatenate([blocks(ba[0]), blocks(bx[0]), blocks(ba[1]), blocks(bx[1])], axis=-1)
    return w, bias.astype(F32)


def _gate_rows(a_log, dt_bias):
    pad = jnp.zeros((LANES - GP_GROUP,), F32)
    row = lambda v: jnp.concatenate([v.reshape(-1).astype(F32), pad]).reshape(1, LANES)
    return row(a_log), row(dt_bias)


def kernel(x, mix_norm, w_in, rg_conv_w, rg_conv_b, rg_wa, rg_ba, rg_wx, rg_bx, rg_lambda, w_rnn_proj,
           dn_conv_w, dn_a_log, dn_dt_bias, dn_norm, w_dn_proj, w_out, ffn_norm, w_gate_up, w_down,
           final_norm):
    b, s, d = x.shape
    depth = w_in.shape[0]
    t = b * s
    assert d == D_MODEL and b % RG_GROUP == 0 and s % (4 * RG_TB) == 0 and s % DN_TR == 0
    assert DN_TR == DN_G * CHUNK
    tm = 512 if t % 512 == 0 else 256
    x2d = x.reshape(t, d)
    for l in range(depth):
        p2d = _in_proj(x2d, mix_norm[l].reshape(1, d), _pack_w_in(w_in[l]), tm=tm)
        p3 = p2d.reshape(b, s, N_P)
        alog_row, dtb_row = _gate_rows(dn_a_log[l], dn_dt_bias[l])
        gp, gt = _gate_prep(p3, alog_row, dtb_row)
        gtr = gt.reshape(b, len(GT_ROWS), 2, DN_HEADS, s // CHUNK, CHUNK)
        gt4 = jnp.concatenate([gtr[:, :, 0], gtr[:, :, 1, :, ::-1]], axis=-1)
        rg_w, rg_bias = _pack_rg_gates(rg_wa[l], rg_wx[l], rg_ba[l], rg_bx[l])
        y_rnn = _rglru(p3, rg_conv_w[l], rg_conv_b[l].reshape(1, D_RNN), rg_w, rg_bias, rg_lambda[l])
        y_dn = _deltanet(p3, gp, gt4, dn_conv_w[l], dn_norm[l].reshape(1, DN_DV))
        x2d = _merge_ffn(
            x2d, y_rnn.reshape(t, D_RNN), y_dn.reshape(t, DN_V), p2d,
            w_rnn_proj[l].astype(BF16), w_dn_proj[l].astype(BF16), w_out[l].astype(BF16),
            ffn_norm[l].reshape(1, d), w_gate_up[l].astype(BF16), w_down[l].astype(BF16),
            final_norm.reshape(1, d), tm=tm, apply_last_norm=(l == depth - 1))
    return x2d.reshape(b, s, d)
```

```python
import functools

import jax
import jax.numpy as jnp
from jax import lax
from jax.experimental import pallas as pl
from jax.experimental.pallas import tpu as pltpu

F32 = jnp.float32
BF16 = jnp.bfloat16

D_MODEL = 1024
D_RNN = 1536
RNN_BLOCK = 128
RNN_BLOCKS = D_RNN // RNN_BLOCK
RG_C = 8.0
CONV_K = 4
DN_HEADS = 8
DN_DK = 128
DN_DV = 128
DN_QK = DN_HEADS * DN_DK
DN_V = DN_HEADS * DN_DV
DN_QKV = 2 * DN_QK + DN_V
CHUNK = 64
D_FF = 2816
EPS = 1e-6
N_BA = 4 * DN_HEADS
LOG2_E = 1.4426950408889634
TINY = 1e-30

LANES = 128
SUBLANES = 8

COL_RX = 0
COL_RY = D_RNN
COL_QKV = 2 * D_RNN
COL_Z = COL_QKV + DN_QKV
COL_GATES = COL_Z + DN_V
COL_BA = COL_GATES + 2 * D_MODEL
N_P = COL_BA + LANES

VMEM_LIMIT = 56 * 1024 * 1024


def _sigmoid(x):
    return 1.0 / (1.0 + jnp.exp(-x))


def _softplus(x):
    return jnp.maximum(x, 0.0) + jnp.log(1.0 + jnp.exp(-jnp.abs(x)))


def _dot(a, b):
    return jnp.dot(a, b, preferred_element_type=F32)


def _dot_nt(a, b):
    return lax.dot_general(a, b, (((1,), (1,)), ((), ())), preferred_element_type=F32)


def _dot_tn(a, b):
    return lax.dot_general(a, b, (((0,), (0,)), ((), ())), preferred_element_type=F32)


def _aligned(start, multiple):
    return start if isinstance(start, int) else pl.multiple_of(start, multiple)


def _lane_sum(x, ones_bf16):
    return _dot(x.astype(BF16), ones_bf16)


CONV_HALO = SUBLANES


def _zero_halo(pad_ref, s):
    zeros = jnp.zeros((CONV_HALO, pad_ref.shape[1]), F32)
    pad_ref[0:CONV_HALO, :] = zeros
    pad_ref[CONV_HALO + s:2 * CONV_HALO + s, :] = zeros


def _dwconv(pad_ref, x, cw):
    s = x.shape[0]
    pad_ref[CONV_HALO:CONV_HALO + s, :] = x
    y = cw[0:1] * pad_ref[CONV_HALO - 2:CONV_HALO - 2 + s, :]
    for j in range(1, CONV_K):
        y = y + cw[j:j + 1] * pad_ref[CONV_HALO - 2 + j:CONV_HALO - 2 + j + s, :]
    return y


def _in_proj_kernel(x_ref, g_ref, w_ref, o_ref, *, col_chunk):
    x = x_ref[...]
    ms = jnp.mean(x * x, axis=-1, keepdims=True)
    h = (x * lax.rsqrt(ms + EPS) * g_ref[...]).astype(BF16)
    n = o_ref.shape[1]
    for c0 in range(0, n, col_chunk):
        cn = min(col_chunk, n - c0)
        o_ref[:, c0:c0 + cn] = _dot(h, w_ref[:, c0:c0 + cn]).astype(o_ref.dtype)


def _in_proj(x2d, gain, w, *, tm):
    t, d = x2d.shape
    n = w.shape[1]
    return pl.pallas_call(
        functools.partial(_in_proj_kernel, col_chunk=1024),
        grid=(t // tm,),
        in_specs=[
            pl.BlockSpec((tm, d), lambda i: (i, 0)),
            pl.BlockSpec((1, d), lambda i: (0, 0)),
            pl.BlockSpec((d, n), lambda i: (0, 0)),
        ],
        out_specs=pl.BlockSpec((tm, n), lambda i: (i, 0)),
        out_shape=jax.ShapeDtypeStruct((t, n), BF16),
        compiler_params=pltpu.CompilerParams(
            dimension_semantics=("arbitrary",), vmem_limit_bytes=VMEM_LIMIT),
        name="in_proj",
    )(x2d, gain, w)


GP_BETA, GP_GC, GP_EGC, GP_BEGC, GP_EGR, GP_EGT = range(6)
GP_GROUP = 2 * DN_HEADS
GT_ROWS = (GP_BETA, GP_GC, GP_BEGC)
GT_BETA, GT_GC, GT_BEGC = range(3)
DN_HB = 2


def _gate_prep_kernel(ba_ref, alog_ref, dtb_ref, gp_ref, gt_ref):
    s = ba_ref.shape[1]
    x = ba_ref[0].astype(F32)
    lane = lax.broadcasted_iota(jnp.int32, (s, LANES), 1)
    row = lax.broadcasted_iota(jnp.int32, (s, LANES), 0)
    ic = row & (CHUNK - 1)
    beta = _sigmoid(x)
    xs = pltpu.roll(x, LANES - GP_GROUP, axis=1)
    g = -jnp.exp(alog_ref[...]) * _softplus(xs + dtb_ref[...])
    fsum = g
    rsum = g
    sh = 1
    while sh < CHUNK:
        fsum = fsum + jnp.where(ic >= sh, pltpu.roll(fsum, sh, axis=0), 0.0)
        rsum = rsum + jnp.where(ic < CHUNK - sh, pltpu.roll(rsum, s - sh, axis=0), 0.0)
        sh *= 2
    is_fwd = (lane & (GP_GROUP - 1)) < DN_HEADS
    gc = jnp.where(is_fwd, fsum, rsum)
    gr = jnp.where(is_fwd, rsum, fsum) - g
    gtot = fsum + rsum - g
    egc = jnp.exp(gc)
    parts = (beta, gc, egc, beta * egc, jnp.exp(gr), jnp.exp(gtot))
    out = parts[-1]
    out = pltpu.roll(out, (len(parts) - 1) * GP_GROUP, axis=1)
    for qi in range(len(parts) - 2, -1, -1):
        placed = parts[qi] if qi == 0 else pltpu.roll(parts[qi], qi * GP_GROUP, axis=1)
        out = jnp.where(lane < (qi + 1) * GP_GROUP, placed, out)
    for p in range(DN_HEADS // DN_HB):
        gp_ref[0, p] = out if p == 0 else pltpu.roll(out, LANES - p * DN_HB, axis=1)
    out_t = out.T
    for n, quantity in enumerate(GT_ROWS):
        gt_ref[0, n] = out_t[quantity * GP_GROUP:(quantity + 1) * GP_GROUP, :]


def _gate_prep(p3, alog_row, dtb_row):
    b, s, _ = p3.shape
    return pl.pallas_call(
        _gate_prep_kernel,
        grid=(b,),
        in_specs=[
            pl.BlockSpec((1, s, LANES), lambda i: (i, 0, COL_BA // LANES)),
            pl.BlockSpec((1, LANES), lambda i: (0, 0)),
            pl.BlockSpec((1, LANES), lambda i: (0, 0)),
        ],
        out_specs=[
            pl.BlockSpec((1, DN_HEADS // DN_HB, s, LANES), lambda i: (i, 0, 0, 0)),
            pl.BlockSpec((1, len(GT_ROWS), GP_GROUP, s), lambda i: (i, 0, 0, 0)),
        ],
        out_shape=[
            jax.ShapeDtypeStruct((b, DN_HEADS // DN_HB, s, LANES), F32),
            jax.ShapeDtypeStruct((b, len(GT_ROWS), GP_GROUP, s), F32),
        ],
        compiler_params=pltpu.CompilerParams(
            dimension_semantics=("arbitrary",), vmem_limit_bytes=VMEM_LIMIT),
        name="gate_prep",
    )(p3, alog_row, dtb_row)


RG_GROUP = SUBLANES
RG_SLAB_PAD = 8
RG_TB = 32


def _rglru_kernel(rx_ref, ry_ref, cw_ref, cb_ref, w_ref, bias_ref, lam_ref, o_ref, u_s, h_s, pad_s,
                  ab0_s, ab1_s):
    s = rx_ref.shape[1]
    pitch = s + RG_SLAB_PAD
    nb = s // RG_TB
    cw_half = 0.5 * cw_ref[...]
    cb_half = 0.5 * cb_ref[...]
    _zero_halo(pad_s, s)

    def stage(b, carry):
        x = rx_ref[b].astype(F32)
        u_s[pl.ds(pl.multiple_of(b * pitch, SUBLANES), s), :] = _dwconv(pad_s, x, cw_half) + cb_half
        return carry

    lax.fori_loop(0, RG_GROUP, stage, 0)

    c2 = (-0.5 * RG_C * LOG2_E) * _softplus(-lam_ref[...])
    bias_half = 0.5 * bias_ref[...]

    def gather(ref, t0):
        return [ref[pl.ds(t0 + j, RG_GROUP, stride=pitch), :] for j in range(RG_TB)]

    def block_start(i, d):
        return (i if d == 0 else nb - 1 - i) * RG_TB

    def coefficients(i, ab_ref):
        i = jnp.minimum(i, nb - 1)
        for d in range(2):
            hu = jnp.concatenate(gather(u_s, block_start(i, d)), axis=0)
            gts = _dot(hu.astype(BF16), w_ref[:, d * 2 * LANES:(d + 1) * 2 * LANES])
            gts = gts + bias_half[:, d * 2 * LANES:(d + 1) * 2 * LANES]
            t_r = jnp.tanh(gts[:, :LANES])
            t_i = jnp.tanh(gts[:, LANES:])
            a = jnp.exp2(c2[d:d + 1] * t_r + c2[d:d + 1])
            om = 1.0 - a * a
            root = om * lax.rsqrt(jnp.maximum(om, TINY))
            ab_ref[d, 0] = a
            ab_ref[d, 1] = root * ((t_i + 1.0) * hu)

    def scan(i, hs, ab_ref, accumulate):
        out = []
        for d in range(2):
            t0 = block_start(i, d)
            h = hs[d]
            order = range(RG_TB) if d == 0 else range(RG_TB - 1, -1, -1)
            for j in order:
                rows = pl.ds(j * RG_GROUP, RG_GROUP)
                h = ab_ref[d, 0, rows, :] * h + ab_ref[d, 1, rows, :]
                dst = pl.ds(t0 + j, RG_GROUP, stride=pitch)
                if accumulate:
                    h_s[dst, :] = h_s[dst, :] + h
                else:
                    h_s[dst, :] = h
            out.append(h)
        return tuple(out)

    def make_body(accumulate):
        def body(k, hs):
            coefficients(2 * k + 1, ab1_s)
            hs = scan(2 * k, hs, ab0_s, accumulate)
            coefficients(2 * k + 2, ab0_s)
            return scan(2 * k + 1, hs, ab1_s, accumulate)
        return body

    zero = jnp.zeros((RG_GROUP, LANES), F32)
    coefficients(0, ab0_s)
    hs = lax.fori_loop(0, nb // 4, make_body(False), (zero, zero))
    lax.fori_loop(nb // 4, nb // 2, make_body(True), hs)

    def finish(b, carry):
        hsum = h_s[pl.ds(pl.multiple_of(b * pitch, SUBLANES), s), :]
        y = hsum * jax.nn.gelu(ry_ref[b].astype(F32))
        o_ref[b] = y.astype(o_ref.dtype)
        return carry

    lax.fori_loop(0, RG_GROUP, finish, 0)


def _rglru(p3, cw, cb, w, bias, lam):
    b, s, _ = p3.shape
    pitch = s + RG_SLAB_PAD
    blk = (RG_GROUP, s, LANES)
    return pl.pallas_call(
        _rglru_kernel,
        grid=(b // RG_GROUP, RNN_BLOCKS),
        in_specs=[
            pl.BlockSpec(blk, lambda g, n: (g, 0, COL_RX // LANES + n)),
            pl.BlockSpec(blk, lambda g, n: (g, 0, COL_RY // LANES + n)),
            pl.BlockSpec((CONV_K, LANES), lambda g, n: (0, n)),
            pl.BlockSpec((1, LANES), lambda g, n: (0, n)),
            pl.BlockSpec((None, RNN_BLOCK, 4 * RNN_BLOCK), lambda g, n: (n, 0, 0)),
            pl.BlockSpec((None, 1, 4 * RNN_BLOCK), lambda g, n: (n, 0, 0)),
            pl.BlockSpec((2, LANES), lambda g, n: (0, n)),
        ],
        out_specs=pl.BlockSpec(blk, lambda g, n: (g, 0, n)),
        out_shape=jax.ShapeDtypeStruct((b, s, D_RNN), BF16),
        scratch_shapes=[
            pltpu.VMEM((RG_GROUP * pitch, LANES), F32),
            pltpu.VMEM((RG_GROUP * pitch, LANES), F32),
            pltpu.VMEM((s + 2 * CONV_HALO, LANES), F32),
            pltpu.VMEM((2, 2, RG_TB * RG_GROUP, LANES), F32),
            pltpu.VMEM((2, 2, RG_TB * RG_GROUP, LANES), F32),
        ],
        compiler_params=pltpu.CompilerParams(
            dimension_semantics=("arbitrary", "arbitrary"), vmem_limit_bytes=VMEM_LIMIT),
        name="rglru",
    )(p3, p3, cw, cb, w, bias, lam)


DN_G = 4
DN_TR = 256
DN_SUB = 128


def _deltanet_kernel(q_ref, k_ref, v_ref, z_ref, gp_ref, gt_ref, cwq_ref, cwk_ref, cwv_ref, ng_ref,
                     o_ref, qn_s, kn_s, v_s, qg_s, kd_s, w_s, u_s, att_s, o_s, pad_s, st_s):
    s = q_ref.shape[1]
    nc = s // CHUNK
    head0 = pl.program_id(1) * DN_HB
    ones = jnp.ones((LANES, LANES), BF16)
    for idx in range(3 * DN_HB):
        _zero_halo(pad_s.at[idx], s)

    def gate_lane(quantity, d, hh):
        return quantity * GP_GROUP + d * DN_HEADS + hh

    def stage_head(hh):
        lanes = slice(hh * LANES, (hh + 1) * LANES)
        for idx, ref in enumerate((q_ref, k_ref, v_ref)):
            pad_s[3 * hh + idx, CONV_HALO:CONV_HALO + s, :] = ref[0, :, lanes].astype(F32)
        cw_half = [0.5 * cw_ref[:, lanes] for cw_ref in (cwq_ref, cwk_ref, cwv_ref)]

        def seq_tile(i, carry=0):
            r0 = _aligned(i * DN_SUB, DN_SUB)
            rows = pl.ds(r0, DN_SUB)

            def conv_silu(idx):
                cw = cw_half[idx]
                pad = pad_s.at[3 * hh + idx]
                hy = cw[0:1] * pad[pl.ds(r0 + CONV_HALO - 2, DN_SUB), :]
                for j in range(1, CONV_K):
                    hy = hy + cw[j:j + 1] * pad[pl.ds(r0 + CONV_HALO - 2 + j, DN_SUB), :]
                return hy * jnp.tanh(hy) + hy

            q = conv_silu(0)
            k = conv_silu(1)
            v = conv_silu(2)
            qn = q * lax.rsqrt(_lane_sum(q * q, ones) + EPS) * (DN_DK ** -0.5)
            kn = k * lax.rsqrt(_lane_sum(k * k, ones) + EPS)
            qn_s[hh, rows, :] = qn.astype(BF16)
            kn_s[hh, rows, :] = kn.astype(BF16)
            v_s[hh, rows, :] = v.astype(BF16)
            gp = gp_ref[0, rows, :]

            def col(quantity, d):
                c = gate_lane(quantity, d, hh)
                return jnp.broadcast_to(gp[:, c:c + 1], (DN_SUB, LANES))

            for d in range(2):
                sid = hh * 2 + d
                qg_s[sid, rows, :] = (qn * col(GP_EGC, d)).astype(BF16)
                kd_s[sid, rows, :] = (kn * col(GP_EGR, d)).astype(BF16)
            return carry

        return seq_tile

    seq_tiles = [stage_head(hh) for hh in range(DN_HB)]

    row = lax.broadcasted_iota(jnp.int32, (CHUNK, 2 * CHUNK), 0)
    lane = lax.broadcasted_iota(jnp.int32, (CHUNK, 2 * CHUNK), 1)
    is_f = lane < CHUNK
    ahead = jnp.where(is_f, row - lane, lane - CHUNK - row)
    incl = ahead >= 0
    strict = ahead > 0
    eye = (ahead == 0).astype(F32)
    keep_f = is_f.astype(F32).astype(BF16)
    keep_b = (1.0 - is_f.astype(F32)).astype(BF16)
    ztile = jnp.zeros((CHUNK, LANES), BF16)

    def block_diag(xb):
        return jnp.concatenate([xb * keep_f, xb * keep_b], axis=0)

    def diag2(f, b):
        return jnp.concatenate([jnp.concatenate([f, ztile], axis=1), jnp.concatenate([ztile, b], axis=1)], axis=0)

    def gate_row(quantity, hh, j):
        return gt_ref[0, quantity, pl.ds(head0 + hh, 1), pl.ds(j, 1), :][0]

    def prep_stages(it):
        pairs = []
        for hh in range(DN_HB):
            for g in range(DN_G):
                j = it * DN_G + g
                pairs.append((hh, j, _aligned(j * CHUNK, CHUNK), _aligned((nc - 1 - j) * CHUNK, CHUNK)))
        xb, tinv = [], []
        for hh, j, rf, rb in pairs:
            knf = kn_s[hh, pl.ds(rf, CHUNK), :]
            knb = kn_s[hh, pl.ds(rb, CHUNK), :]
            kcat = jnp.concatenate([knf, knb], axis=1)
            qcat = jnp.concatenate([qn_s[hh, pl.ds(rf, CHUNK), :], qn_s[hh, pl.ds(rb, CHUNK), :]], axis=1)
            kdiag = diag2(knf, knb)
            kk = _dot_nt(kcat, kdiag)
            qk = _dot_nt(qcat, kdiag)

            def gcol(quantity):
                lf = gate_lane(quantity, 0, hh)
                lb = gate_lane(quantity, 1, hh)
                f = jnp.broadcast_to(gp_ref[0, pl.ds(rf, CHUNK), lf:lf + 1], (CHUNK, 2 * CHUNK))
                b = jnp.broadcast_to(gp_ref[0, pl.ds(rb, CHUNK), lb:lb + 1], (CHUNK, 2 * CHUNK))
                return jnp.where(is_f, f, b)

            grow = gate_row(GT_GC, hh, j)
            decay = jnp.where(incl, jnp.exp(gcol(GP_GC) - grow), 0.0)
            att_s[hh, pl.ds(rf, CHUNK), :] = (qk * decay).astype(BF16)
            x = jnp.where(strict, -(gcol(GP_BETA) * kk * decay), 0.0)
            xb.append(x.astype(BF16))
            tinv.append(eye + x)
        diag = [block_diag(x) for x in xb]
        yield
        xb = [_dot(x, m).astype(BF16) for x, m in zip(xb, diag)]
        yield
        for k in range(1, 6):
            diag = [block_diag(x) for x in xb]
            if k < 5:
                prod = [_dot(jnp.concatenate([x, t.astype(BF16)], axis=0), m)
                        for x, t, m in zip(xb, tinv, diag)]
                xb = [p[:CHUNK].astype(BF16) for p in prod]
                tinv = [t + p[CHUNK:] for t, p in zip(tinv, prod)]
            else:
                tinv = [t + _dot(t.astype(BF16), m) for t, m in zip(tinv, diag)]
            yield
        for n, (hh, j, rf, rb) in enumerate(pairs):
            sf, sb = hh * 2, hh * 2 + 1
            t_u = (tinv[n] * gate_row(GT_BETA, hh, j)).astype(BF16)
            t_w = (tinv[n] * gate_row(GT_BEGC, hh, j)).astype(BF16)
            u = _dot(t_u, diag2(v_s[hh, pl.ds(rf, CHUNK), :], v_s[hh, pl.ds(rb, CHUNK), :]))
            w = _dot(t_w, diag2(kn_s[hh, pl.ds(rf, CHUNK), :], kn_s[hh, pl.ds(rb, CHUNK), :]))
            u_s[sf, pl.ds(rf, CHUNK), :] = u[:, :DN_DV]
            u_s[sb, pl.ds(rb, CHUNK), :] = u[:, DN_DV:]
            w_s[sf, pl.ds(rf, CHUNK), :] = w[:, :DN_DK].astype(BF16)
            w_s[sb, pl.ds(rb, CHUNK), :] = w[:, DN_DK:].astype(BF16)
        yield

    streams = [(hh, d) for hh in range(DN_HB) for d in range(2)]

    def chunk_rows(i):
        return [_aligned((i if d == 0 else nc - 1 - i) * CHUNK, CHUNK) for _, d in streams]

    def chain_read(i, states):
        res = []
        for (hh, d), r0 in zip(streams, chunk_rows(i)):
            sid = hh * 2 + d
            wq = jnp.concatenate([w_s[sid, pl.ds(r0, CHUNK), :], qg_s[sid, pl.ds(r0, CHUNK), :]], axis=0)
            res.append(_dot(wq, st_s[sid].astype(BF16)))
        return res

    def chain_update(i, states, res):
        for (hh, d), r0, rs in zip(streams, chunk_rows(i), res):
            sid = hh * 2 + d
            v_new = (u_s[sid, pl.ds(r0, CHUNK), :] - rs[:CHUNK]).astype(BF16)
            att = att_s[hh, pl.ds(_aligned(i * CHUNK, CHUNK), CHUNK), :]
            v_pad = jnp.concatenate([v_new, ztile] if d == 0 else [ztile, v_new], axis=0)
            o_s[sid, pl.ds(r0, CHUNK), :] = rs[CHUNK:] + _dot(att, v_pad)
            egt_lane = gate_lane(GP_EGT, d, hh)
            egt = gp_ref[0, pl.ds(r0, 1), egt_lane:egt_lane + 1]
            st_s[sid] = st_s[sid] * egt + _dot_tn(kd_s[sid, pl.ds(r0, CHUNK), :], v_new)
        return states

    def chain_group(it, states, *fillers):
        def advance():
            for f in fillers:
                next(f, None)
        for g in range(DN_G):
            res = chain_read(it * DN_G + g, states)
            advance()
            states = chain_update(it * DN_G + g, states, res)
            advance()
        for f in fillers:
            for _ in f:
                pass
        return states

    def out_tile(i, carry=0):
        rows = pl.ds(_aligned(i * DN_SUB, DN_SUB), DN_SUB)
        for hh in range(DN_HB):
            lanes = slice(hh * LANES, (hh + 1) * LANES)
            o = o_s[hh * 2, rows, :] + o_s[hh * 2 + 1, rows, :]
            var = _lane_sum(o * o, ones) * (1.0 / DN_DV)
            hz = 0.5 * z_ref[0, rows, lanes].astype(F32)
            y = o * lax.rsqrt(var + EPS) * ng_ref[...] * (hz * jnp.tanh(hz) + hz)
            o_ref[0, rows, lanes] = y.astype(o_ref.dtype)
        return carry

    n = nc // DN_G
    subs = DN_TR // DN_SUB

    def seq_items(g):
        if g <= n - 1 - g:
            for t in sorted({g, n - 1 - g}):
                for sub in range(t * subs, (t + 1) * subs):
                    for tile in seq_tiles:
                        tile(sub)
                        yield

    def out_items(g):
        if 0 <= n - 1 - g <= g:
            for t in sorted({g, n - 1 - g}):
                for sub in range(t * subs, (t + 1) * subs):
                    out_tile(sub)
                    yield

    def run(*generators):
        live = list(generators)
        while live:
            live = [g for g in live if next(g, StopIteration) is not StopIteration]

    run(seq_items(0))
    run(prep_stages(0), seq_items(1))
    st_s[...] = jnp.zeros(st_s.shape, F32)
    states = None
    for g in range(n):
        fillers = [seq_items(g + 2), out_items(g - 1)]
        if g + 1 < n:
            fillers.insert(0, prep_stages(g + 1))
        states = chain_group(g, states, *fillers)
    run(out_items(n - 1))


def _deltanet(p3, gp, gt4, cw, ng):
    b, s, _ = p3.shape
    width = DN_HB * LANES
    blk = (1, s, width)
    hblocks = DN_HEADS // DN_HB
    qkv0 = COL_QKV // width
    ns = 2 * DN_HB
    seq_bf16 = lambda: pltpu.VMEM((ns, s, LANES), BF16)
    seq_f32 = lambda: pltpu.VMEM((ns, s, LANES), F32)
    return pl.pallas_call(
        _deltanet_kernel,
        grid=(b, hblocks),
        in_specs=[
            pl.BlockSpec(blk, lambda i, h: (i, 0, qkv0 + h)),
            pl.BlockSpec(blk, lambda i, h: (i, 0, qkv0 + hblocks + h)),
            pl.BlockSpec(blk, lambda i, h: (i, 0, qkv0 + 2 * hblocks + h)),
            pl.BlockSpec(blk, lambda i, h: (i, 0, COL_Z // width + h)),
            pl.BlockSpec((1, None, s, LANES), lambda i, h: (i, h, 0, 0)),
            pl.BlockSpec((1, len(GT_ROWS), DN_HEADS, s // CHUNK, 2 * CHUNK), lambda i, h: (i, 0, 0, 0, 0)),
            pl.BlockSpec((CONV_K, width), lambda i, h: (0, h)),
            pl.BlockSpec((CONV_K, width), lambda i, h: (0, hblocks + h)),
            pl.BlockSpec((CONV_K, width), lambda i, h: (0, 2 * hblocks + h)),
            pl.BlockSpec((1, LANES), lambda i, h: (0, 0)),
        ],
        out_specs=pl.BlockSpec(blk, lambda i, h: (i, 0, h)),
        out_shape=jax.ShapeDtypeStruct((b, s, DN_V), BF16),
        scratch_shapes=[
            pltpu.VMEM((DN_HB, s, LANES), BF16),
            pltpu.VMEM((DN_HB, s, LANES), BF16),
            pltpu.VMEM((DN_HB, s, LANES), BF16),
            seq_bf16(),
            seq_bf16(),
            seq_bf16(),
            seq_f32(),
            pltpu.VMEM((DN_HB, s, 2 * CHUNK), BF16),
            seq_f32(),
            pltpu.VMEM((3 * DN_HB, s + 2 * CONV_HALO, LANES), F32),
            pltpu.VMEM((ns, DN_DK, DN_DV), F32),
        ],
        compiler_params=pltpu.CompilerParams(
            dimension_semantics=("arbitrary", "arbitrary"), vmem_limit_bytes=VMEM_LIMIT),
        name="deltanet",
    )(p3, p3, p3, p3, gp, gt4, cw, cw, cw, ng)


FF_CHUNK = 256


def _merge_ffn_kernel(x_ref, yr_ref, yd_ref, gr_ref, gd_ref, wr_ref, wd_ref, wo_ref, fn_ref, wgu_ref,
                      wdn_ref, last_ref, o_ref, *, apply_last_norm):
    y_rnn = _dot(yr_ref[...], wr_ref[...])
    y_dn = _dot(yd_ref[...], wd_ref[...])
    merged = (_sigmoid(gr_ref[...].astype(F32)) * y_rnn + _sigmoid(gd_ref[...].astype(F32)) * y_dn)
    x1 = x_ref[...] + _dot(merged.astype(BF16), wo_ref[...])
    ms = jnp.mean(x1 * x1, axis=-1, keepdims=True)
    h = (x1 * lax.rsqrt(ms + EPS) * fn_ref[...]).astype(BF16)
    acc = x1
    for c0 in range(0, D_FF, FF_CHUNK):
        gt = _dot(h, wgu_ref[:, c0:c0 + FF_CHUNK])
        up = _dot(h, wgu_ref[:, D_FF + c0:D_FF + c0 + FF_CHUNK])
        act = (gt * _sigmoid(gt) * up).astype(BF16)
        acc = acc + _dot(act, wdn_ref[c0:c0 + FF_CHUNK, :])
    if apply_last_norm:
        ms = jnp.mean(acc * acc, axis=-1, keepdims=True)
        acc = acc * lax.rsqrt(ms + EPS) * last_ref[...]
    o_ref[...] = acc


def _merge_ffn(x2d, y_rnn, y_dn, p2d, w_rnn, w_dn, w_out, ffn_gain, w_gu, w_down, last_gain, *, tm,
               apply_last_norm):
    t, d = x2d.shape
    full = lambda a: pl.BlockSpec(a.shape, lambda i: (0,) * a.ndim)
    return pl.pallas_call(
        functools.partial(_merge_ffn_kernel, apply_last_norm=apply_last_norm),
        grid=(t // tm,),
        in_specs=[
            pl.BlockSpec((tm, d), lambda i: (i, 0)),
            pl.BlockSpec((tm, D_RNN), lambda i: (i, 0)),
            pl.BlockSpec((tm, DN_V), lambda i: (i, 0)),
            pl.BlockSpec((tm, d), lambda i: (i, COL_GATES // D_MODEL)),
            pl.BlockSpec((tm, d), lambda i: (i, COL_GATES // D_MODEL + 1)),
            full(w_rnn), full(w_dn), full(w_out), full(ffn_gain), full(w_gu), full(w_down), full(last_gain),
        ],
        out_specs=pl.BlockSpec((tm, d), lambda i: (i, 0)),
        out_shape=jax.ShapeDtypeStruct((t, d), F32),
        compiler_params=pltpu.CompilerParams(
            dimension_semantics=("arbitrary",), vmem_limit_bytes=VMEM_LIMIT),
        name="merge_ffn",
    )(x2d, y_rnn, y_dn, p2d, p2d, w_rnn, w_dn, w_out, ffn_gain, w_gu, w_down, last_gain)


def _pack_w_in(w_in):
    split_ba = COL_Z + DN_V
    w = w_in.astype(BF16)
    pad = jnp.zeros(w.shape[:-1] + (LANES - N_BA,), BF16)
    return jnp.concatenate(
        [w[..., :split_ba], w[..., split_ba + N_BA:], w[..., split_ba:split_ba + N_BA], pad], axis=-1)


def _pack_rg_gates(wa, wx, ba, bx):
    w = jnp.concatenate([wa[0], wx[0], wa[1], wx[1]], axis=-1).astype(BF16)
    blocks = lambda v: v.reshape(RNN_BLOCKS, 1, RNN_BLOCK)
    bias = jnp.concatenate([blocks(ba[0]), blocks(bx[0]), blocks(ba[1]), blocks(bx[1])], axis=-1)
    return w, bias.astype(F32)


def _gate_rows(a_log, dt_bias):
    pad = jnp.zeros((LANES - GP_GROUP,), F32)
    row = lambda v: jnp.concatenate([v.reshape(-1).astype(F32), pad]).reshape(1, LANES)
    return row(a_log), row(dt_bias)


def kernel(x, mix_norm, w_in, rg_conv_w, rg_conv_b, rg_wa, rg_ba, rg_wx, rg_bx, rg_lambda, w_rnn_proj,
           dn_conv_w, dn_a_log, dn_dt_bias, dn_norm, w_dn_proj, w_out, ffn_norm, w_gate_up, w_down,
           final_norm):
    b, s, d = x.shape
    depth = w_in.shape[0]
    t = b * s
    assert d == D_MODEL and b % RG_GROUP == 0 and s % (4 * RG_TB) == 0 and s % DN_TR == 0
    assert DN_TR == DN_G * CHUNK
    tm = 512 if t % 512 == 0 else 256
    x2d = x.reshape(t, d)
    w_in_packed = _pack_w_in(w_in)
    for l in range(depth):
        p2d = _in_proj(x2d, mix_norm[l].reshape(1, d), w_in_packed[l], tm=tm)
        p3 = p2d.reshape(b, s, N_P)
        alog_row, dtb_row = _gate_rows(dn_a_log[l], dn_dt_bias[l])
        gp, gt = _gate_prep(p3, alog_row, dtb_row)
        gtr = gt.reshape(b, len(GT_ROWS), 2, DN_HEADS, s // CHUNK, CHUNK)
        gt4 = jnp.concatenate([gtr[:, :, 0], gtr[:, :, 1, :, ::-1]], axis=-1)
        rg_w, rg_bias = _pack_rg_gates(rg_wa[l], rg_wx[l], rg_ba[l], rg_bx[l])
        y_rnn = _rglru(p3, rg_conv_w[l], rg_conv_b[l].reshape(1, D_RNN), rg_w, rg_bias, rg_lambda[l])
        y_dn = _deltanet(p3, gp, gt4, dn_conv_w[l], dn_norm[l].reshape(1, DN_DV))
        x2d = _merge_ffn(
            x2d, y_rnn.reshape(t, D_RNN), y_dn.reshape(t, DN_V), p2d,
            w_rnn_proj[l].astype(BF16), w_dn_proj[l].astype(BF16), w_out[l].astype(BF16),
            ffn_norm[l].reshape(1, d), w_gate_up[l].astype(BF16), w_down[l].astype(BF16),
            final_norm.reshape(1, d), tm=tm, apply_last_norm=(l == depth - 1))
    return x2d.reshape(b, s, d)
```

```python
import functools

import jax
import jax.numpy as jnp
from jax import lax
from jax.experimental import pallas as pl
from jax.experimental.pallas import tpu as pltpu

F32 = jnp.float32
BF16 = jnp.bfloat16

D_MODEL = 1024
D_RNN = 1536
RNN_BLOCK = 128
RNN_BLOCKS = D_RNN // RNN_BLOCK
RG_C = 8.0
CONV_K = 4
DN_HEADS = 8
DN_DK = 128
DN_DV = 128
DN_QK = DN_HEADS * DN_DK
DN_V = DN_HEADS * DN_DV
DN_QKV = 2 * DN_QK + DN_V
CHUNK = 64
D_FF = 2816
EPS = 1e-6
N_BA = 4 * DN_HEADS
LOG2_E = 1.4426950408889634
TINY = 1e-30

LANES = 128
SUBLANES = 8

COL_RX = 0
COL_RY = D_RNN
COL_QKV = 2 * D_RNN
COL_Z = COL_QKV + DN_QKV
COL_GATES = COL_Z + DN_V
COL_BA = COL_GATES + 2 * D_MODEL
N_P = COL_BA + LANES

VMEM_LIMIT = 56 * 1024 * 1024


def _sigmoid(x):
    return 1.0 / (1.0 + jnp.exp(-x))


def _softplus(x):
    return jnp.maximum(x, 0.0) + jnp.log(1.0 + jnp.exp(-jnp.abs(x)))


def _dot(a, b):
    return jnp.dot(a, b, preferred_element_type=F32)


def _dot_nt(a, b):
    return lax.dot_general(a, b, (((1,), (1,)), ((), ())), preferred_element_type=F32)


def _dot_tn(a, b):
    return lax.dot_general(a, b, (((0,), (0,)), ((), ())), preferred_element_type=F32)


def _aligned(start, multiple):
    return start if isinstance(start, int) else pl.multiple_of(start, multiple)


def _lane_sum(x, ones_bf16):
    return _dot(x.astype(BF16), ones_bf16)


CONV_HALO = SUBLANES


def _zero_halo(pad_ref, s):
    zeros = jnp.zeros((CONV_HALO, pad_ref.shape[1]), F32)
    pad_ref[0:CONV_HALO, :] = zeros
    pad_ref[CONV_HALO + s:2 * CONV_HALO + s, :] = zeros


def _dwconv(pad_ref, x, cw):
    s = x.shape[0]
    pad_ref[CONV_HALO:CONV_HALO + s, :] = x
    y = cw[0:1] * pad_ref[CONV_HALO - 2:CONV_HALO - 2 + s, :]
    for j in range(1, CONV_K):
        y = y + cw[j:j + 1] * pad_ref[CONV_HALO - 2 + j:CONV_HALO - 2 + j + s, :]
    return y


def _in_proj_kernel(x_ref, g_ref, w_ref, o_ref, *, col_chunk):
    x = x_ref[...]
    ms = jnp.mean(x * x, axis=-1, keepdims=True)
    h = (x * lax.rsqrt(ms + EPS) * g_ref[...]).astype(BF16)
    n = o_ref.shape[1]
    for c0 in range(0, n, col_chunk):
        cn = min(col_chunk, n - c0)
        o_ref[:, c0:c0 + cn] = _dot(h, w_ref[:, c0:c0 + cn]).astype(o_ref.dtype)


def _in_proj(x2d, gain, w, *, tm):
    t, d = x2d.shape
    n = w.shape[1]
    return pl.pallas_call(
        functools.partial(_in_proj_kernel, col_chunk=1024),
        grid=(t // tm,),
        in_specs=[
            pl.BlockSpec((tm, d), lambda i: (i, 0)),
            pl.BlockSpec((1, d), lambda i: (0, 0)),
            pl.BlockSpec((d, n), lambda i: (0, 0)),
        ],
        out_specs=pl.BlockSpec((tm, n), lambda i: (i, 0)),
        out_shape=jax.ShapeDtypeStruct((t, n), BF16),
        compiler_params=pltpu.CompilerParams(
            dimension_semantics=("arbitrary",), vmem_limit_bytes=VMEM_LIMIT),
        name="in_proj",
    )(x2d, gain, w)


GP_BETA, GP_GC, GP_EGC, GP_BEGC, GP_EGR, GP_EGT = range(6)
GP_GROUP = 2 * DN_HEADS
GT_ROWS = (GP_BETA, GP_GC, GP_BEGC)
GT_BETA, GT_GC, GT_BEGC = range(3)
DN_HB = 2


def _gate_prep_kernel(ba_ref, alog_ref, dtb_ref, gp_ref, gt_ref):
    s = ba_ref.shape[1]
    x = ba_ref[0].astype(F32)
    lane = lax.broadcasted_iota(jnp.int32, (s, LANES), 1)
    row = lax.broadcasted_iota(jnp.int32, (s, LANES), 0)
    ic = row & (CHUNK - 1)
    beta = _sigmoid(x)
    xs = pltpu.roll(x, LANES - GP_GROUP, axis=1)
    g = -jnp.exp(alog_ref[...]) * _softplus(xs + dtb_ref[...])
    fsum = g
    rsum = g
    sh = 1
    while sh < CHUNK:
        fsum = fsum + jnp.where(ic >= sh, pltpu.roll(fsum, sh, axis=0), 0.0)
        rsum = rsum + jnp.where(ic < CHUNK - sh, pltpu.roll(rsum, s - sh, axis=0), 0.0)
        sh *= 2
    is_fwd = (lane & (GP_GROUP - 1)) < DN_HEADS
    gc = jnp.where(is_fwd, fsum, rsum)
    gr = jnp.where(is_fwd, rsum, fsum) - g
    gtot = fsum + rsum - g
    egc = jnp.exp(gc)
    parts = (beta, gc, egc, beta * egc, jnp.exp(gr), jnp.exp(gtot))
    out = parts[-1]
    out = pltpu.roll(out, (len(parts) - 1) * GP_GROUP, axis=1)
    for qi in range(len(parts) - 2, -1, -1):
        placed = parts[qi] if qi == 0 else pltpu.roll(parts[qi], qi * GP_GROUP, axis=1)
        out = jnp.where(lane < (qi + 1) * GP_GROUP, placed, out)
    for p in range(DN_HEADS // DN_HB):
        gp_ref[0, p] = out if p == 0 else pltpu.roll(out, LANES - p * DN_HB, axis=1)
    out_t = out.T
    for n, quantity in enumerate(GT_ROWS):
        gt_ref[0, n] = out_t[quantity * GP_GROUP:(quantity + 1) * GP_GROUP, :]


def _gate_prep(p3, alog_row, dtb_row):
    b, s, _ = p3.shape
    return pl.pallas_call(
        _gate_prep_kernel,
        grid=(b,),
        in_specs=[
            pl.BlockSpec((1, s, LANES), lambda i: (i, 0, COL_BA // LANES)),
            pl.BlockSpec((1, LANES), lambda i: (0, 0)),
            pl.BlockSpec((1, LANES), lambda i: (0, 0)),
        ],
        out_specs=[
            pl.BlockSpec((1, DN_HEADS // DN_HB, s, LANES), lambda i: (i, 0, 0, 0)),
            pl.BlockSpec((1, len(GT_ROWS), GP_GROUP, s), lambda i: (i, 0, 0, 0)),
        ],
        out_shape=[
            jax.ShapeDtypeStruct((b, DN_HEADS // DN_HB, s, LANES), F32),
            jax.ShapeDtypeStruct((b, len(GT_ROWS), GP_GROUP, s), F32),
        ],
        compiler_params=pltpu.CompilerParams(
            dimension_semantics=("arbitrary",), vmem_limit_bytes=VMEM_LIMIT),
        name="gate_prep",
    )(p3, alog_row, dtb_row)


RG_GROUP = SUBLANES
RG_SLAB_PAD = 8
RG_TB = 32


def _rglru_kernel(rx_ref, ry_ref, cw_ref, cb_ref, w_ref, bias_ref, lam_ref, o_ref, u_s, h_s, pad_s,
                  ab0_s, ab1_s):
    s = rx_ref.shape[1]
    pitch = s + RG_SLAB_PAD
    nb = s // RG_TB
    cw_half = 0.5 * cw_ref[...]
    cb_half = 0.5 * cb_ref[...]
    _zero_halo(pad_s, s)

    def stage(b, carry):
        x = rx_ref[b].astype(F32)
        u_s[pl.ds(pl.multiple_of(b * pitch, SUBLANES), s), :] = _dwconv(pad_s, x, cw_half) + cb_half
        return carry

    lax.fori_loop(0, RG_GROUP, stage, 0)

    c2 = (-0.5 * RG_C * LOG2_E) * _softplus(-lam_ref[...])
    bias_half = 0.5 * bias_ref[...]

    def gather(ref, t0):
        return [ref[pl.ds(t0 + j, RG_GROUP, stride=pitch), :] for j in range(RG_TB)]

    def block_start(i, d):
        return (i if d == 0 else nb - 1 - i) * RG_TB

    def coefficients(i, ab_ref):
        i = jnp.minimum(i, nb - 1)
        for d in range(2):
            hu = jnp.concatenate(gather(u_s, block_start(i, d)), axis=0)
            gts = _dot(hu.astype(BF16), w_ref[:, d * 2 * LANES:(d + 1) * 2 * LANES])
            gts = gts + bias_half[:, d * 2 * LANES:(d + 1) * 2 * LANES]
            t_r = jnp.tanh(gts[:, :LANES])
            t_i = jnp.tanh(gts[:, LANES:])
            a = jnp.exp2(c2[d:d + 1] * t_r + c2[d:d + 1])
            om = 1.0 - a * a
            root = om * lax.rsqrt(jnp.maximum(om, TINY))
            ab_ref[d, 0] = a
            ab_ref[d, 1] = root * ((t_i + 1.0) * hu)

    def scan(i, hs, ab_ref, accumulate):
        out = []
        for d in range(2):
            t0 = block_start(i, d)
            h = hs[d]
            order = range(RG_TB) if d == 0 else range(RG_TB - 1, -1, -1)
            for j in order:
                rows = pl.ds(j * RG_GROUP, RG_GROUP)
                h = ab_ref[d, 0, rows, :] * h + ab_ref[d, 1, rows, :]
                dst = pl.ds(t0 + j, RG_GROUP, stride=pitch)
                if accumulate:
                    h_s[dst, :] = h_s[dst, :] + h
                else:
                    h_s[dst, :] = h
            out.append(h)
        return tuple(out)

    def make_body(accumulate):
        def body(k, hs):
            coefficients(2 * k + 1, ab1_s)
            hs = scan(2 * k, hs, ab0_s, accumulate)
            coefficients(2 * k + 2, ab0_s)
            return scan(2 * k + 1, hs, ab1_s, accumulate)
        return body

    zero = jnp.zeros((RG_GROUP, LANES), F32)
    coefficients(0, ab0_s)
    hs = lax.fori_loop(0, nb // 4, make_body(False), (zero, zero))
    lax.fori_loop(nb // 4, nb // 2, make_body(True), hs)

    def finish(b, carry):
        hsum = h_s[pl.ds(pl.multiple_of(b * pitch, SUBLANES), s), :]
        y = hsum * jax.nn.gelu(ry_ref[b].astype(F32))
        o_ref[b] = y.astype(o_ref.dtype)
        return carry

    lax.fori_loop(0, RG_GROUP, finish, 0)


def _rglru(p3, cw, cb, w, bias, lam):
    b, s, _ = p3.shape
    pitch = s + RG_SLAB_PAD
    blk = (RG_GROUP, s, LANES)
    return pl.pallas_call(
        _rglru_kernel,
        grid=(b // RG_GROUP, RNN_BLOCKS),
        in_specs=[
            pl.BlockSpec(blk, lambda g, n: (g, 0, COL_RX // LANES + n)),
            pl.BlockSpec(blk, lambda g, n: (g, 0, COL_RY // LANES + n)),
            pl.BlockSpec((CONV_K, LANES), lambda g, n: (0, n)),
            pl.BlockSpec((1, LANES), lambda g, n: (0, n)),
            pl.BlockSpec((None, RNN_BLOCK, 4 * RNN_BLOCK), lambda g, n: (n, 0, 0)),
            pl.BlockSpec((None, 1, 4 * RNN_BLOCK), lambda g, n: (n, 0, 0)),
            pl.BlockSpec((2, LANES), lambda g, n: (0, n)),
        ],
        out_specs=pl.BlockSpec(blk, lambda g, n: (g, 0, n)),
        out_shape=jax.ShapeDtypeStruct((b, s, D_RNN), BF16),
        scratch_shapes=[
            pltpu.VMEM((RG_GROUP * pitch, LANES), F32),
            pltpu.VMEM((RG_GROUP * pitch, LANES), F32),
            pltpu.VMEM((s + 2 * CONV_HALO, LANES), F32),
            pltpu.VMEM((2, 2, RG_TB * RG_GROUP, LANES), F32),
            pltpu.VMEM((2, 2, RG_TB * RG_GROUP, LANES), F32),
        ],
        compiler_params=pltpu.CompilerParams(
            dimension_semantics=("arbitrary", "arbitrary"), vmem_limit_bytes=VMEM_LIMIT),
        name="rglru",
    )(p3, p3, cw, cb, w, bias, lam)


DN_G = 4
DN_TR = 256
DN_SUB = 128


def _deltanet_kernel(q_ref, k_ref, v_ref, z_ref, gp_ref, gt_ref, cwq_ref, cwk_ref, cwv_ref, ng_ref,
                     o_ref, qn_s, kn_s, v_s, qg_s, kd_s, w_s, u_s, att_s, o_s, pad_s, st_s):
    s = q_ref.shape[1]
    nc = s // CHUNK
    head0 = pl.program_id(1) * DN_HB
    ones = jnp.ones((LANES, LANES), BF16)
    for idx in range(3 * DN_HB):
        _zero_halo(pad_s.at[idx], s)

    def gate_lane(quantity, d, hh):
        return quantity * GP_GROUP + d * DN_HEADS + hh

    def stage_head(hh):
        lanes = slice(hh * LANES, (hh + 1) * LANES)
        for idx, ref in enumerate((q_ref, k_ref, v_ref)):
            pad_s[3 * hh + idx, CONV_HALO:CONV_HALO + s, :] = ref[0, :, lanes].astype(F32)
        cw_half = [0.5 * cw_ref[:, lanes] for cw_ref in (cwq_ref, cwk_ref, cwv_ref)]

        def seq_tile(i, carry=0):
            r0 = _aligned(i * DN_SUB, DN_SUB)
            rows = pl.ds(r0, DN_SUB)

            def conv_silu(idx):
                cw = cw_half[idx]
                pad = pad_s.at[3 * hh + idx]
                hy = cw[0:1] * pad[pl.ds(r0 + CONV_HALO - 2, DN_SUB), :]
                for j in range(1, CONV_K):
                    hy = hy + cw[j:j + 1] * pad[pl.ds(r0 + CONV_HALO - 2 + j, DN_SUB), :]
                return hy * jnp.tanh(hy) + hy

            q = conv_silu(0)
            k = conv_silu(1)
            v = conv_silu(2)
            qn = q * lax.rsqrt(_lane_sum(q * q, ones) + EPS) * (DN_DK ** -0.5)
            kn = k * lax.rsqrt(_lane_sum(k * k, ones) + EPS)
            qn_s[hh, rows, :] = qn.astype(BF16)
            kn_s[hh, rows, :] = kn.astype(BF16)
            v_s[hh, rows, :] = v.astype(BF16)
            gp = gp_ref[0, rows, :]

            def col(quantity, d):
                c = gate_lane(quantity, d, hh)
                return jnp.broadcast_to(gp[:, c:c + 1], (DN_SUB, LANES))

            for d in range(2):
                sid = hh * 2 + d
                qg_s[sid, rows, :] = (qn * col(GP_EGC, d)).astype(BF16)
                kd_s[sid, rows, :] = (kn * col(GP_EGR, d)).astype(BF16)
            return carry

        return seq_tile

    seq_tiles = [stage_head(hh) for hh in range(DN_HB)]

    row = lax.broadcasted_iota(jnp.int32, (CHUNK, 2 * CHUNK), 0)
    lane = lax.broadcasted_iota(jnp.int32, (CHUNK, 2 * CHUNK), 1)
    is_f = lane < CHUNK
    ahead = jnp.where(is_f, row - lane, lane - CHUNK - row)
    incl = ahead >= 0
    strict = ahead > 0
    eye = (ahead == 0).astype(F32)
    keep_f = is_f.astype(F32).astype(BF16)
    keep_b = (1.0 - is_f.astype(F32)).astype(BF16)
    ztile = jnp.zeros((CHUNK, LANES), BF16)

    def block_diag(xb):
        return jnp.concatenate([xb * keep_f, xb * keep_b], axis=0)

    def diag2(f, b):
        return jnp.concatenate([jnp.concatenate([f, ztile], axis=1), jnp.concatenate([ztile, b], axis=1)], axis=0)

    def gate_row(quantity, hh, j):
        return gt_ref[0, quantity, pl.ds(head0 + hh, 1), pl.ds(j, 1), :][0]

    def prep_stages(it):
        pairs = []
        for hh in range(DN_HB):
            for g in range(DN_G):
                j = it * DN_G + g
                pairs.append((hh, j, _aligned(j * CHUNK, CHUNK), _aligned((nc - 1 - j) * CHUNK, CHUNK)))
        xb, tinv = [], []
        for hh, j, rf, rb in pairs:
            knf = kn_s[hh, pl.ds(rf, CHUNK), :]
            knb = kn_s[hh, pl.ds(rb, CHUNK), :]
            kcat = jnp.concatenate([knf, knb], axis=1)
            qcat = jnp.concatenate([qn_s[hh, pl.ds(rf, CHUNK), :], qn_s[hh, pl.ds(rb, CHUNK), :]], axis=1)
            kdiag = diag2(knf, knb)
            kq = _dot_nt(jnp.concatenate([kcat, qcat], axis=0), kdiag)
            kk = kq[:CHUNK]
            qk = kq[CHUNK:]

            def gcol(quantity):
                lf = gate_lane(quantity, 0, hh)
                lb = gate_lane(quantity, 1, hh)
                f = jnp.broadcast_to(gp_ref[0, pl.ds(rf, CHUNK), lf:lf + 1], (CHUNK, 2 * CHUNK))
                b = jnp.broadcast_to(gp_ref[0, pl.ds(rb, CHUNK), lb:lb + 1], (CHUNK, 2 * CHUNK))
                return jnp.where(is_f, f, b)

            grow = gate_row(GT_GC, hh, j)
            decay = jnp.where(incl, jnp.exp(gcol(GP_GC) - grow), 0.0)
            att_s[hh, pl.ds(rf, CHUNK), :] = (qk * decay).astype(BF16)
            x = jnp.where(strict, -(gcol(GP_BETA) * kk * decay), 0.0)
            xb.append(x.astype(BF16))
            tinv.append(eye + x)
        diag = [block_diag(x) for x in xb]
        yield
        xb = [_dot(x, m).astype(BF16) for x, m in zip(xb, diag)]
        yield
        for k in range(1, 6):
            diag = [block_diag(x) for x in xb]
            if k < 5:
                prod = [_dot(jnp.concatenate([x, t.astype(BF16)], axis=0), m)
                        for x, t, m in zip(xb, tinv, diag)]
                xb = [p[:CHUNK].astype(BF16) for p in prod]
                tinv = [t + p[CHUNK:] for t, p in zip(tinv, prod)]
            else:
                tinv = [t + _dot(t.astype(BF16), m) for t, m in zip(tinv, diag)]
            yield
        for n, (hh, j, rf, rb) in enumerate(pairs):
            sf, sb = hh * 2, hh * 2 + 1
            t_u = (tinv[n] * gate_row(GT_BETA, hh, j)).astype(BF16)
            t_w = (tinv[n] * gate_row(GT_BEGC, hh, j)).astype(BF16)
            u = _dot(t_u, diag2(v_s[hh, pl.ds(rf, CHUNK), :], v_s[hh, pl.ds(rb, CHUNK), :]))
            w = _dot(t_w, diag2(kn_s[hh, pl.ds(rf, CHUNK), :], kn_s[hh, pl.ds(rb, CHUNK), :]))
            u_s[sf, pl.ds(rf, CHUNK), :] = u[:, :DN_DV]
            u_s[sb, pl.ds(rb, CHUNK), :] = u[:, DN_DV:]
            w_s[sf, pl.ds(rf, CHUNK), :] = w[:, :DN_DK].astype(BF16)
            w_s[sb, pl.ds(rb, CHUNK), :] = w[:, DN_DK:].astype(BF16)
        yield

    streams = [(hh, d) for hh in range(DN_HB) for d in range(2)]

    def chunk_rows(i):
        return [_aligned((i if d == 0 else nc - 1 - i) * CHUNK, CHUNK) for _, d in streams]

    def chain_read(i, states):
        res = []
        for (hh, d), r0 in zip(streams, chunk_rows(i)):
            sid = hh * 2 + d
            wq = jnp.concatenate([w_s[sid, pl.ds(r0, CHUNK), :], qg_s[sid, pl.ds(r0, CHUNK), :]], axis=0)
            res.append(_dot(wq, st_s[sid].astype(BF16)))
        return res

    def chain_update(i, states, res):
        for (hh, d), r0, rs in zip(streams, chunk_rows(i), res):
            sid = hh * 2 + d
            v_new = (u_s[sid, pl.ds(r0, CHUNK), :] - rs[:CHUNK]).astype(BF16)
            att = att_s[hh, pl.ds(_aligned(i * CHUNK, CHUNK), CHUNK), :]
            v_pad = jnp.concatenate([v_new, ztile] if d == 0 else [ztile, v_new], axis=0)
            o_s[sid, pl.ds(r0, CHUNK), :] = rs[CHUNK:] + _dot(att, v_pad)
            egt_lane = gate_lane(GP_EGT, d, hh)
            egt = gp_ref[0, pl.ds(r0, 1), egt_lane:egt_lane + 1]
            st_s[sid] = st_s[sid] * egt + _dot_tn(kd_s[sid, pl.ds(r0, CHUNK), :], v_new)
        return states

    def chain_group(it, states, *fillers):
        def advance():
            for f in fillers:
                next(f, None)
        for g in range(DN_G):
            res = chain_read(it * DN_G + g, states)
            advance()
            states = chain_update(it * DN_G + g, states, res)
            advance()
        for f in fillers:
            for _ in f:
                pass
        return states

    def out_tile(i, carry=0):
        rows = pl.ds(_aligned(i * DN_SUB, DN_SUB), DN_SUB)
        for hh in range(DN_HB):
            lanes = slice(hh * LANES, (hh + 1) * LANES)
            o = o_s[hh * 2, rows, :] + o_s[hh * 2 + 1, rows, :]
            var = _lane_sum(o * o, ones) * (1.0 / DN_DV)
            hz = 0.5 * z_ref[0, rows, lanes].astype(F32)
            y = o * lax.rsqrt(var + EPS) * ng_ref[...] * (hz * jnp.tanh(hz) + hz)
            o_ref[0, rows, lanes] = y.astype(o_ref.dtype)
        return carry

    n = nc // DN_G
    subs = DN_TR // DN_SUB

    def seq_items(g):
        if g <= n - 1 - g:
            for t in sorted({g, n - 1 - g}):
                for sub in range(t * subs, (t + 1) * subs):
                    for tile in seq_tiles:
                        tile(sub)
                        yield

    def out_items(g):
        if 0 <= n - 1 - g <= g:
            for t in sorted({g, n - 1 - g}):
                for sub in range(t * subs, (t + 1) * subs):
                    out_tile(sub)
                    yield

    def run(*generators):
        live = list(generators)
        while live:
            live = [g for g in live if next(g, StopIteration) is not StopIteration]

    run(seq_items(0))
    run(prep_stages(0), seq_items(1))
    st_s[...] = jnp.zeros(st_s.shape, F32)
    states = None
    for g in range(n):
        fillers = [seq_items(g + 2), out_items(g - 1)]
        if g + 1 < n:
            fillers.insert(0, prep_stages(g + 1))
        states = chain_group(g, states, *fillers)
    run(out_items(n - 1))


def _deltanet(p3, gp, gt4, cw, ng):
    b, s, _ = p3.shape
    width = DN_HB * LANES
    blk = (1, s, width)
    hblocks = DN_HEADS // DN_HB
    qkv0 = COL_QKV // width
    ns = 2 * DN_HB
    seq_bf16 = lambda: pltpu.VMEM((ns, s, LANES), BF16)
    seq_f32 = lambda: pltpu.VMEM((ns, s, LANES), F32)
    return pl.pallas_call(
        _deltanet_kernel,
        grid=(b, hblocks),
        in_specs=[
            pl.BlockSpec(blk, lambda i, h: (i, 0, qkv0 + h)),
            pl.BlockSpec(blk, lambda i, h: (i, 0, qkv0 + hblocks + h)),
            pl.BlockSpec(blk, lambda i, h: (i, 0, qkv0 + 2 * hblocks + h)),
            pl.BlockSpec(blk, lambda i, h: (i, 0, COL_Z // width + h)),
            pl.BlockSpec((1, None, s, LANES), lambda i, h: (i, h, 0, 0)),
            pl.BlockSpec((1, len(GT_ROWS), DN_HEADS, s // CHUNK, 2 * CHUNK), lambda i, h: (i, 0, 0, 0, 0)),
            pl.BlockSpec((CONV_K, width), lambda i, h: (0, h)),
            pl.BlockSpec((CONV_K, width), lambda i, h: (0, hblocks + h)),
            pl.BlockSpec((CONV_K, width), lambda i, h: (0, 2 * hblocks + h)),
            pl.BlockSpec((1, LANES), lambda i, h: (0, 0)),
        ],
        out_specs=pl.BlockSpec(blk, lambda i, h: (i, 0, h)),
        out_shape=jax.ShapeDtypeStruct((b, s, DN_V), BF16),
        scratch_shapes=[
            pltpu.VMEM((DN_HB, s, LANES), BF16),
            pltpu.VMEM((DN_HB, s, LANES), BF16),
            pltpu.VMEM((DN_HB, s, LANES), BF16),
            seq_bf16(),
            seq_bf16(),
            seq_bf16(),
            seq_f32(),
            pltpu.VMEM((DN_HB, s, 2 * CHUNK), BF16),
            seq_f32(),
            pltpu.VMEM((3 * DN_HB, s + 2 * CONV_HALO, LANES), F32),
            pltpu.VMEM((ns, DN_DK, DN_DV), F32),
        ],
        compiler_params=pltpu.CompilerParams(
            dimension_semantics=("arbitrary", "arbitrary"), vmem_limit_bytes=VMEM_LIMIT),
        name="deltanet",
    )(p3, p3, p3, p3, gp, gt4, cw, cw, cw, ng)


FF_CHUNK = 256


def _merge_ffn_kernel(x_ref, yr_ref, yd_ref, gr_ref, gd_ref, wr_ref, wd_ref, wo_ref, fn_ref, wgu_ref,
                      wdn_ref, last_ref, o_ref, *, apply_last_norm):
    y_rnn = _dot(yr_ref[...], wr_ref[...])
    y_dn = _dot(yd_ref[...], wd_ref[...])
    merged = (_sigmoid(gr_ref[...].astype(F32)) * y_rnn + _sigmoid(gd_ref[...].astype(F32)) * y_dn)
    x1 = x_ref[...] + _dot(merged.astype(BF16), wo_ref[...])
    ms = jnp.mean(x1 * x1, axis=-1, keepdims=True)
    h = (x1 * lax.rsqrt(ms + EPS) * fn_ref[...]).astype(BF16)
    acc = x1
    for c0 in range(0, D_FF, FF_CHUNK):
        gt = _dot(h, wgu_ref[:, c0:c0 + FF_CHUNK])
        up = _dot(h, wgu_ref[:, D_FF + c0:D_FF + c0 + FF_CHUNK])
        act = (gt * _sigmoid(gt) * up).astype(BF16)
        acc = acc + _dot(act, wdn_ref[c0:c0 + FF_CHUNK, :])
    if apply_last_norm:
        ms = jnp.mean(acc * acc, axis=-1, keepdims=True)
        acc = acc * lax.rsqrt(ms + EPS) * last_ref[...]
    o_ref[...] = acc


def _merge_ffn(x2d, y_rnn, y_dn, p2d, w_rnn, w_dn, w_out, ffn_gain, w_gu, w_down, last_gain, *, tm,
               apply_last_norm):
    t, d = x2d.shape
    full = lambda a: pl.BlockSpec(a.shape, lambda i: (0,) * a.ndim)
    return pl.pallas_call(
        functools.partial(_merge_ffn_kernel, apply_last_norm=apply_last_norm),
        grid=(t // tm,),
        in_specs=[
            pl.BlockSpec((tm, d), lambda i: (i, 0)),
            pl.BlockSpec((tm, D_RNN), lambda i: (i, 0)),
            pl.BlockSpec((tm, DN_V), lambda i: (i, 0)),
            pl.BlockSpec((tm, d), lambda i: (i, COL_GATES // D_MODEL)),
            pl.BlockSpec((tm, d), lambda i: (i, COL_GATES // D_MODEL + 1)),
            full(w_rnn), full(w_dn), full(w_out), full(ffn_gain), full(w_gu), full(w_down), full(last_gain),
        ],
        out_specs=pl.BlockSpec((tm, d), lambda i: (i, 0)),
        out_shape=jax.ShapeDtypeStruct((t, d), F32),
        compiler_params=pltpu.CompilerParams(
            dimension_semantics=("arbitrary",), vmem_limit_bytes=VMEM_LIMIT),
        name="merge_ffn",
    )(x2d, y_rnn, y_dn, p2d, p2d, w_rnn, w_dn, w_out, ffn_gain, w_gu, w_down, last_gain)


def _pack_w_in(w_in):
    split_ba = COL_Z + DN_V
    w = w_in.astype(BF16)
    pad = jnp.zeros(w.shape[:-1] + (LANES - N_BA,), BF16)
    return jnp.concatenate(
        [w[..., :split_ba], w[..., split_ba + N_BA:], w[..., split_ba:split_ba + N_BA], pad], axis=-1)


def _pack_rg_gates(wa, wx, ba, bx):
    w = jnp.concatenate([wa[0], wx[0], wa[1], wx[1]], axis=-1).astype(BF16)
    blocks = lambda v: v.reshape(RNN_BLOCKS, 1, RNN_BLOCK)
    bias = jnp.concatenate([blocks(ba[0]), blocks(bx[0]), blocks(ba[1]), blocks(bx[1])], axis=-1)
    return w, bias.astype(F32)


def _gate_rows(a_log, dt_bias):
    pad = jnp.zeros((LANES - GP_GROUP,), F32)
    row = lambda v: jnp.concatenate([v.reshape(-1).astype(F32), pad]).reshape(1, LANES)
    return row(a_log), row(dt_bias)


def kernel(x, mix_norm, w_in, rg_conv_w, rg_conv_b, rg_wa, rg_ba, rg_wx, rg_bx, rg_lambda, w_rnn_proj,
           dn_conv_w, dn_a_log, dn_dt_bias, dn_norm, w_dn_proj, w_out, ffn_norm, w_gate_up, w_down,
           final_norm):
    b, s, d = x.shape
    depth = w_in.shape[0]
    t = b * s
    assert d == D_MODEL and b % RG_GROUP == 0 and s % (4 * RG_TB) == 0 and s % DN_TR == 0
    assert DN_TR == DN_G * CHUNK
    tm = 512 if t % 512 == 0 else 256
    x2d = x.reshape(t, d)
    w_in_packed = _pack_w_in(w_in)
    for l in range(depth):
        p2d = _in_proj(x2d, mix_norm[l].reshape(1, d), w_in_packed[l], tm=tm)
        p3 = p2d.reshape(b, s, N_P)
        alog_row, dtb_row = _gate_rows(dn_a_log[l], dn_dt_bias[l])
        gp, gt = _gate_prep(p3, alog_row, dtb_row)
        gtr = gt.reshape(b, len(GT_ROWS), 2, DN_HEADS, s // CHUNK, CHUNK)
        gt4 = jnp.concatenate([gtr[:, :, 0], gtr[:, :, 1, :, ::-1]], axis=-1)
        rg_w, rg_bias = _pack_rg_gates(rg_wa[l], rg_wx[l], rg_ba[l], rg_bx[l])
        y_rnn = _rglru(p3, rg_conv_w[l], rg_conv_b[l].reshape(1, D_RNN), rg_w, rg_bias, rg_lambda[l])
        y_dn = _deltanet(p3, gp, gt4, dn_conv_w[l], dn_norm[l].reshape(1, DN_DV))
        x2d = _merge_ffn(
            x2d, y_rnn.reshape(t, D_RNN), y_dn.reshape(t, DN_V), p2d,
            w_rnn_proj[l].astype(BF16), w_dn_proj[l].astype(BF16), w_out[l].astype(BF16),
            ffn_norm[l].reshape(1, d), w_gate_up[l].astype(BF16), w_down[l].astype(BF16),
            final_norm.reshape(1, d), tm=tm, apply_last_norm=(l == depth - 1))
    return x2d.reshape(b, s, d)
```
